```python
import math
import jax, jax.numpy as jnp
from jax import lax
import numpy as np

D_MODEL = 1024
BATCH = 32
SEQ = 2048
DEPTH = 2
DEC_BATCH = 128
DEC_SEQ = 8
PAST_LEN = 16384
PAGE_SIZE = 128

PLE_DIM = 256
Q_BLOCK = 128
HEAD_DIM = 64
EPS = 1e-6
A_HEADS = 8
IDX_HEADS = 8
IDX_DIM = 64
TOPK_MAX = 256
N_BUCKETS = 32
MAX_DISTANCE = 128
B_HEADS = 8
Q_LORA = 256
KV_LORA = 128
NOPE_DIM = 64
ROPE_DIM = 32
B_V_DIM = 64
ROPE_THETA = 10000.0
MLA_SCALE = (NOPE_DIM + ROPE_DIM) ** -0.5
C_HEADS = 8
C_KV_HEADS = 2
C_GROUP = C_HEADS // C_KV_HEADS
N_KEYS = 128
N_EXPERTS = N_KEYS * N_KEYS
PEER_HEADS = 8
PEER_KEY_DIM = 128
PEER_TOPK = 16
PEER_CHUNK = 128
IN_SIZES = (A_HEADS * HEAD_DIM, HEAD_DIM, HEAD_DIM, IDX_HEADS * IDX_DIM, IDX_HEADS, IDX_DIM,
            Q_LORA, KV_LORA, ROPE_DIM,
            C_HEADS * HEAD_DIM, C_KV_HEADS * HEAD_DIM, C_KV_HEADS * HEAD_DIM,
            3 * D_MODEL)
IN_SPLITS = tuple(int(s) for s in np.cumsum(IN_SIZES)[:-1])
D_IN = sum(IN_SIZES)

kernel_name = 'hybrid_dsa_mla_stickbreak_peer_step'

F32 = jnp.float32


def rmsnorm(x, g):
    xf = x.astype(F32)
    y = xf * lax.rsqrt(jnp.mean(xf * xf, axis=-1, keepdims=True) + EPS)
    return (y * g.astype(F32)).astype(x.dtype)


def rope(x, pos):
    half = ROPE_DIM // 2
    inv = ROPE_THETA ** (-jnp.arange(half, dtype=F32) / half)
    ang = pos.astype(F32)[:, None] * inv
    ang = ang.reshape((1, ang.shape[0]) + (1,) * (x.ndim - 3) + (half,))
    cos, sin = jnp.cos(ang), jnp.sin(ang)
    xf = x.astype(F32)
    x1, x2 = xf[..., :half], xf[..., half:]
    return jnp.concatenate([x1 * cos - x2 * sin, x2 * cos + x1 * sin], axis=-1).astype(x.dtype)


def t5_bucket(dist):
    n = jnp.maximum(dist, 0)
    exact = N_BUCKETS // 2
    big = exact + (jnp.log(jnp.maximum(n, 1).astype(F32) / exact) / math.log(MAX_DISTANCE / exact)
                   * (N_BUCKETS - exact)).astype(jnp.int32)
    return jnp.where(n < exact, n, jnp.minimum(big, N_BUCKETS - 1))


def to_blocks(t):
    b, s = t.shape[0], t.shape[1]
    return t.reshape((b, s // Q_BLOCK, Q_BLOCK) + t.shape[2:]).swapaxes(0, 1)


def from_blocks(t):
    nb, b, qb = t.shape[0], t.shape[1], t.shape[2]
    return t.swapaxes(0, 1).reshape((b, nb * qb) + t.shape[3:])


def paged_rows(cache, i, page_table):
    r = cache[i, page_table]
    return r.reshape((r.shape[0], r.shape[1] * r.shape[2]) + r.shape[3:])


def gather_selected(cache, i, page_table, new_rows, sel):
    bidx = jnp.arange(sel.shape[0])[:, None, None]
    past_pos = jnp.minimum(sel, PAST_LEN - 1)
    phys = page_table[bidx, past_pos // PAGE_SIZE]
    past = cache[i, phys, past_pos % PAGE_SIZE]
    new = new_rows[bidx, jnp.clip(sel - PAST_LEN, 0, DEC_SEQ - 1)]
    return jnp.where((sel < PAST_LEN)[..., None], past, new)


def project(n, pos, w_in, g_q, w_uq, g_kv, w_uk):
    bs, t = n.shape[0], n.shape[1]
    (aq, ak, av, iq, iw, ik, bq, bkv, bkr, cq, ck, cv, gates) = jnp.split(n @ w_in, IN_SPLITS, axis=-1)
    aq = aq.reshape(bs, t, A_HEADS, HEAD_DIM)
    iq = iq.reshape(bs, t, IDX_HEADS, IDX_DIM)
    qb = (rmsnorm(bq, g_q) @ w_uq).reshape(bs, t, B_HEADS, NOPE_DIM + ROPE_DIM)
    q_lat = jnp.einsum('bthd,chd->bthc', qb[..., :NOPE_DIM], w_uk)
    q_rope = rope(qb[..., NOPE_DIM:], pos)
    c = rmsnorm(bkv, g_kv)
    kr = rope(bkr, pos)
    cq = cq.reshape(bs, t, C_KV_HEADS, C_GROUP, HEAD_DIM)
    ck = ck.reshape(bs, t, C_KV_HEADS, HEAD_DIM)
    cv = cv.reshape(bs, t, C_KV_HEADS, HEAD_DIM)
    return aq, ak, av, iq, iw, ik, q_lat, q_rope, c, kr, cq, ck, cv, gates


def index_scores(iq, iw, ik):
    s = jnp.einsum('bthd,bsd->bths', iq, ik).astype(F32) * IDX_DIM ** -0.5
    return jnp.einsum('bth,bths->bts', iw.astype(F32) * IDX_HEADS ** -0.5, jax.nn.relu(s))


def dsa_attend(q, k_sel, v_sel, pos_sel, valid, pos_q, rel_bias):
    logits = jnp.einsum('bthd,btkd->bthk', q, k_sel).astype(F32) * HEAD_DIM ** -0.5
    bias = rel_bias[t5_bucket(pos_q[None, :, None] - pos_sel)].astype(F32)
    logits = jnp.where(valid[:, :, None, :], logits + jnp.moveaxis(bias, -1, 2), -jnp.inf)
    p = jax.nn.softmax(logits, axis=-1).astype(v_sel.dtype)
    return jnp.einsum('bthk,btkd->bthd', p, v_sel)


def mla_attend(q_lat, q_rope, c, kr, pos_q, pos_k):
    s = (jnp.einsum('bthc,bsc->bhts', q_lat, c) + jnp.einsum('bthr,bsr->bhts', q_rope, kr)).astype(F32) * MLA_SCALE
    s = jnp.where(pos_k[None, :] <= pos_q[:, None], s, -jnp.inf)
    p = jax.nn.softmax(s, axis=-1).astype(c.dtype)
    return jnp.einsum('bhts,bsc->bthc', p, c)


def sb_attend(q, k, v, pos_q, pos_k):
    z = jnp.einsum('btngd,bsnd->bngts', q, k).astype(F32) * HEAD_DIM ** -0.5
    strict = pos_k[None, :] < pos_q[:, None]
    log_keep = jnp.where(strict, jax.nn.log_sigmoid(-z), 0.0)
    after = lax.cumsum(log_keep, axis=z.ndim - 1, reverse=True) - log_keep
    a = jnp.where(strict, jnp.exp(jax.nn.log_sigmoid(z) + after), 0.0).astype(v.dtype)
    return jnp.einsum('bngts,bsnd->btngd', a, v)


def prompt_mixers(aq, ak, av, iq, iw, ik, q_lat, q_rope, c, kr, cq, ck, cv, rel_bias):
    bs = aq.shape[0]
    kpos = jnp.arange(SEQ)
    k_sel = min(TOPK_MAX, SEQ // 4)
    bidx = jnp.arange(bs)[:, None, None]
    n_blocks = SEQ // Q_BLOCK

    def body(args):
        aqb, iqb, iwb, qlb, qrb, cqb, start = args
        posq = start + jnp.arange(Q_BLOCK)
        causal = kpos[None, :] <= posq[:, None]
        scores = jnp.where(causal[None], index_scores(iqb, iwb, ik), -jnp.inf)
        _, sel = lax.top_k(scores, k_sel)
        oa = dsa_attend(aqb, ak[bidx, sel], av[bidx, sel], sel, sel <= posq[None, :, None], posq, rel_bias)
        ob = mla_attend(qlb, qrb, c, kr, posq, kpos)
        oc = sb_attend(cqb, ck, cv, posq, kpos)
        return oa, ob, oc

    xs = (to_blocks(aq), to_blocks(iq), to_blocks(iw), to_blocks(q_lat), to_blocks(q_rope), to_blocks(cq),
          jnp.arange(n_blocks) * Q_BLOCK)
    oa, ob, oc = lax.map(body, xs)
    return from_blocks(oa), from_blocks(ob), from_blocks(oc)


def sample_mixers(aq, ak, av, iq, iw, ik, q_lat, q_rope, c, kr, cq, ck, cv, i, page_table,
                  cache_a_k, cache_a_v, cache_a_idx, cache_b_ckv, cache_b_krope, cache_c_k, cache_c_v, rel_bias):
    n_keys = PAST_LEN + DEC_SEQ
    kpos = jnp.arange(n_keys)
    posq = PAST_LEN + jnp.arange(DEC_SEQ)
    causal = kpos[None, :] <= posq[:, None]
    ik_all = jnp.concatenate([paged_rows(cache_a_idx, i, page_table), ik], axis=1)
    scores = jnp.where(causal[None], index_scores(iq, iw, ik_all), -jnp.inf)
    _, sel = lax.top_k(scores, min(TOPK_MAX, n_keys // 4))
    k_sel = gather_selected(cache_a_k, i, page_table, ak, sel)
    v_sel = gather_selected(cache_a_v, i, page_table, av, sel)
    oa = dsa_attend(aq, k_sel, v_sel, sel, sel <= posq[None, :, None], posq, rel_bias)
    c_all = jnp.concatenate([paged_rows(cache_b_ckv, i, page_table), c], axis=1)
    kr_all = jnp.concatenate([paged_rows(cache_b_krope, i, page_table), kr], axis=1)
    ob = mla_attend(q_lat, q_rope, c_all, kr_all, posq, kpos)
    ck_all = jnp.concatenate([paged_rows(cache_c_k, i, page_table), ck], axis=1)
    cv_all = jnp.concatenate([paged_rows(cache_c_v, i, page_table), cv], axis=1)
    oc = sb_attend(cq, ck_all, cv_all, posq, kpos)
    return oa, ob, oc


def merge(gates, oa, ob_lat, oc, w_uv, w_br_a, w_br_b, w_br_c, w_o):
    bs, t = oa.shape[0], oa.shape[1]
    ob = jnp.einsum('bthc,chd->bthd', ob_lat, w_uv).reshape(bs, t, -1)
    ga, gb, gc = jnp.split(gates, 3, axis=-1)
    m = (jax.nn.sigmoid(ga) * (oa.reshape(bs, t, -1) @ w_br_a)
         + jax.nn.sigmoid(gb) * (ob @ w_br_b)
         + jax.nn.sigmoid(gc) * (oc.reshape(bs, t, -1) @ w_br_c))
    return m @ w_o


def peer(x, w_pq, sub_k1, sub_k2, peer_u, peer_v):
    shp = x.shape
    xt = x.reshape(-1, D_MODEL)
    n_tok = xt.shape[0]
    n_chunks = -(-n_tok // PEER_CHUNK)
    xt = jnp.pad(xt, ((0, n_chunks * PEER_CHUNK - n_tok), (0, 0))).reshape(n_chunks, PEER_CHUNK, D_MODEL)

    def chunk(xc):
        q = (xc @ w_pq).reshape(PEER_CHUNK, PEER_HEADS, 2, PEER_KEY_DIM)
        s1 = jnp.einsum('chd,kd->chk', q[:, :, 0], sub_k1).astype(F32)
        s2 = jnp.einsum('chd,kd->chk', q[:, :, 1], sub_k2).astype(F32)
        v1, i1 = lax.top_k(s1, PEER_TOPK)
        v2, i2 = lax.top_k(s2, PEER_TOPK)
        cand = (v1[..., :, None] + v2[..., None, :]).reshape(PEER_CHUNK, PEER_HEADS, PEER_TOPK * PEER_TOPK)
        cidx = (i1[..., :, None] * N_KEYS + i2[..., None, :]).reshape(PEER_CHUNK, PEER_HEADS, PEER_TOPK * PEER_TOPK)
        sc, j = lax.top_k(cand, PEER_TOPK)
        e = jnp.take_along_axis(cidx, j, axis=-1)
        g = jax.nn.softmax(sc, axis=-1)
        a = jnp.einsum('chkd,cd->chk', peer_u[e], xc).astype(F32)
        w = (g * jax.nn.gelu(a)).astype(xc.dtype)
        return jnp.einsum('chk,chkd->cd', w, peer_v[e])

    out = lax.map(chunk, xt).reshape(-1, D_MODEL)[:n_tok]
    return out.reshape(shp)


def channel_and_ple(h, p, g_ffn, w_pq, sub_k1, sub_k2, peer_u, peer_v, g_ple, w_pg, w_pp):
    h = h + peer(rmsnorm(h, g_ffn), w_pq, sub_k1, sub_k2, peer_u, peer_v)
    return h + jax.nn.sigmoid(rmsnorm(h, g_ple) @ w_pg) * (p @ w_pp)


def setup_inputs(seed: int = 0) -> dict:
    key = jax.random.key(seed)
    ks = iter(jax.random.split(key, 48))

    def nrm(shape, scale=1.0):
        return jax.random.normal(next(ks), shape, F32) * scale

    def gain(shape):
        return 1.0 + 0.01 * nrm(shape)

    n_pages = PAST_LEN // PAGE_SIZE
    n_pool = (DEC_BATCH * n_pages * 5) // 4
    x_prompt = nrm((BATCH, SEQ, D_MODEL))
    x_sample = nrm((DEC_BATCH, DEC_SEQ, D_MODEL))
    cache_a_k = nrm((DEPTH, n_pool, PAGE_SIZE, HEAD_DIM))
    cache_a_v = nrm((DEPTH, n_pool, PAGE_SIZE, HEAD_DIM))
    cache_a_idx = nrm((DEPTH, n_pool, PAGE_SIZE, IDX_DIM))
    cache_b_ckv = nrm((DEPTH, n_pool, PAGE_SIZE, KV_LORA))
    cache_b_krope = nrm((DEPTH, n_pool, PAGE_SIZE, ROPE_DIM))
    cache_c_k = nrm((DEPTH, n_pool, PAGE_SIZE, C_KV_HEADS, HEAD_DIM))
    cache_c_v = nrm((DEPTH, n_pool, PAGE_SIZE, C_KV_HEADS, HEAD_DIM))
    page_table = jax.random.permutation(next(ks), n_pool)[:DEC_BATCH * n_pages].reshape(DEC_BATCH, n_pages).astype(jnp.int32)
    p_prompt = nrm((DEPTH, BATCH, SEQ, PLE_DIM))
    p_sample = nrm((DEPTH, DEC_BATCH, DEC_SEQ, PLE_DIM))
    return {
        'x_prompt': x_prompt, 'x_sample': x_sample,
        'cache_a_k': cache_a_k, 'cache_a_v': cache_a_v, 'cache_a_idx': cache_a_idx,
        'cache_b_ckv': cache_b_ckv, 'cache_b_krope': cache_b_krope,
        'cache_c_k': cache_c_k, 'cache_c_v': cache_c_v,
        'page_table': page_table, 'p_prompt': p_prompt, 'p_sample': p_sample,
        'rel_bias': nrm((N_BUCKETS, A_HEADS), 0.5),
        'g_mix': gain((DEPTH, D_MODEL)),
        'w_in': nrm((DEPTH, D_MODEL, D_IN), D_MODEL ** -0.5),
        'g_q': gain((DEPTH, Q_LORA)),
        'w_uq': nrm((DEPTH, Q_LORA, B_HEADS * (NOPE_DIM + ROPE_DIM)), Q_LORA ** -0.5),
        'g_kv': gain((DEPTH, KV_LORA)),
        'w_uk': nrm((DEPTH, KV_LORA, B_HEADS, NOPE_DIM), KV_LORA ** -0.5),
        'w_uv': nrm((DEPTH, KV_LORA, B_HEADS, B_V_DIM), KV_LORA ** -0.5),
        'w_br_a': nrm((DEPTH, A_HEADS * HEAD_DIM, D_MODEL), (A_HEADS * HEAD_DIM) ** -0.5),
        'w_br_b': nrm((DEPTH, B_HEADS * B_V_DIM, D_MODEL), (B_HEADS * B_V_DIM) ** -0.5),
        'w_br_c': nrm((DEPTH, C_HEADS * HEAD_DIM, D_MODEL), (C_HEADS * HEAD_DIM) ** -0.5),
        'w_o': nrm((DEPTH, D_MODEL, D_MODEL), D_MODEL ** -0.5),
        'g_ffn': gain((DEPTH, D_MODEL)),
        'w_pq': nrm((DEPTH, D_MODEL, PEER_HEADS * 2 * PEER_KEY_DIM), D_MODEL ** -0.5),
        'sub_k1': nrm((DEPTH, N_KEYS, PEER_KEY_DIM), PEER_KEY_DIM ** -0.5),
        'sub_k2': nrm((DEPTH, N_KEYS, PEER_KEY_DIM), PEER_KEY_DIM ** -0.5),
        'peer_u': nrm((DEPTH, N_EXPERTS, D_MODEL), D_MODEL ** -0.5),
        'peer_v': nrm((DEPTH, N_EXPERTS, D_MODEL), PEER_HEADS ** -0.5),
        'g_ple': gain((DEPTH, D_MODEL)),
        'w_pg': nrm((DEPTH, D_MODEL, D_MODEL), D_MODEL ** -0.5),
        'w_pp': nrm((DEPTH, PLE_DIM, D_MODEL), PLE_DIM ** -0.5),
        'g_final': gain((D_MODEL,)),
    }


def reference(x_prompt, x_sample, cache_a_k, cache_a_v, cache_a_idx, cache_b_ckv, cache_b_krope,
              cache_c_k, cache_c_v, page_table, p_prompt, p_sample, rel_bias, g_mix, w_in, g_q, w_uq,
              g_kv, w_uk, w_uv, w_br_a, w_br_b, w_br_c, w_o, g_ffn, w_pq, sub_k1, sub_k2, peer_u, peer_v,
              g_ple, w_pg, w_pp, g_final):
    hp, hs = x_prompt, x_sample
    rows_p = [[] for _ in range(7)]
    rows_s = [[] for _ in range(7)]
    pos_p = jnp.arange(SEQ)
    pos_s = PAST_LEN + jnp.arange(DEC_SEQ)
    for i in range(DEPTH):
        pr = project(rmsnorm(hp, g_mix[i]), pos_p, w_in[i], g_q[i], w_uq[i], g_kv[i], w_uk[i])
        oa, ob, oc = prompt_mixers(*pr[:13], rel_bias)
        hp = hp + merge(pr[13], oa, ob, oc, w_uv[i], w_br_a[i], w_br_b[i], w_br_c[i], w_o[i])
        hp = channel_and_ple(hp, p_prompt[i], g_ffn[i], w_pq[i], sub_k1[i], sub_k2[i], peer_u[i], peer_v[i],
                             g_ple[i], w_pg[i], w_pp[i])
        for lst, r in zip(rows_p, (pr[1], pr[2], pr[5], pr[8], pr[9], pr[11], pr[12])):
            lst.append(r)
        sr = project(rmsnorm(hs, g_mix[i]), pos_s, w_in[i], g_q[i], w_uq[i], g_kv[i], w_uk[i])
        oa, ob, oc = sample_mixers(*sr[:13], i, page_table, cache_a_k, cache_a_v, cache_a_idx,
                                   cache_b_ckv, cache_b_krope, cache_c_k, cache_c_v, rel_bias)
        hs = hs + merge(sr[13], oa, ob, oc, w_uv[i], w_br_a[i], w_br_b[i], w_br_c[i], w_o[i])
        hs = channel_and_ple(hs, p_sample[i], g_ffn[i], w_pq[i], sub_k1[i], sub_k2[i], peer_u[i], peer_v[i],
                             g_ple[i], w_pg[i], w_pp[i])
        for lst, r in zip(rows_s, (sr[1], sr[2], sr[5], sr[8], sr[9], sr[11], sr[12])):
            lst.append(r)
    y_prompt = rmsnorm(hp, g_final)
    y_sample = rmsnorm(hs, g_final)
    a_k_p, a_v_p, a_idx_p, b_ckv_p, b_krope_p, c_k_p, c_v_p = [jnp.stack(r, axis=0) for r in rows_p]
    a_k_s, a_v_s, a_idx_s, b_ckv_s, b_krope_s, c_k_s, c_v_s = [jnp.stack(r, axis=0) for r in rows_s]
    return (y_prompt, y_sample,
            a_k_p, a_v_p, a_idx_p, b_ckv_p, b_krope_p, c_k_p, c_v_p,
            a_k_s, a_v_s, a_idx_s, b_ckv_s, b_krope_s, c_k_s, c_v_s)
```

```python
import functools
import math

import jax
import jax.numpy as jnp
import numpy as np
from jax import lax
from jax.experimental import pallas as pl
from jax.experimental.pallas import tpu as pltpu

F32 = jnp.float32
BF16 = jnp.bfloat16
I32 = jnp.int32

D_MODEL = 1024
HEAD_DIM = 64
EPS = 1e-6
N_HEADS = 8
IDX_DIM = 64
TOPK_MAX = 256
N_BUCKETS = 32
MAX_DISTANCE = 128
Q_LORA = 256
KV_LORA = 128
NOPE_DIM = 64
ROPE_DIM = 32
ROPE_THETA = 10000.0
MLA_SCALE = (NOPE_DIM + ROPE_DIM) ** -0.5
IDX_SCALE = IDX_DIM ** -0.5 * N_HEADS ** -0.5
C_KV_HEADS = 2
C_GROUP = 4
N_KEYS = 128
PEER_TOPK = 16
PEER_PAIRS = N_HEADS * PEER_TOPK
PLE_DIM = 256

NEG = -1e30
INT_MIN = -(2 ** 31)
VMEM_LIMIT = 56 * 1024 * 1024

C_AQ, C_AKV, C_IQ, C_IKW, C_BQ, C_BKV, C_BKR, C_CQ, C_CK, C_CV, C_GATE, C_END = (
    0, 512, 640, 1152, 1280, 1536, 1664, 1792, 2304, 2432, 2560, 5632)

NT = (((1,), (1,)), ((), ()))


def _cparams(sem):
    return pltpu.CompilerParams(dimension_semantics=sem, vmem_limit_bytes=VMEM_LIMIT)


def _rms(x, g):
    return x * lax.rsqrt(jnp.mean(x * x, axis=-1, keepdims=True) + EPS) * g


def _dot(a, b):
    return jnp.dot(a, b, preferred_element_type=F32)


def _dot_nt(a, b):
    return lax.dot_general(a, b, NT, preferred_element_type=F32)


def _project_kernel(x_ref, gmix_ref, w_ref, gq_ref, wq2_ref, gkv_ref, rc_ref, rsa_ref, rsb_ref,
                    aq_ref, ak_ref, av_ref, iq_ref, ik_ref, iw_ref, qb_ref, ckv_ref, kr_ref, kb_ref,
                    cq_ref, ck_ref, cv_ref, gate_ref):
    n = _rms(x_ref[...], gmix_ref[...]).astype(BF16)

    def seg(a, b):
        return _dot(n, w_ref[:, a:b])

    aq_ref[...] = (seg(C_AQ, C_AKV) * HEAD_DIM ** -0.5).astype(BF16)
    t = seg(C_AKV, C_IQ)
    ak_ref[...] = t[:, :HEAD_DIM]
    av_ref[...] = t[:, HEAD_DIM:]
    iq_ref[...] = seg(C_IQ, C_IKW).astype(BF16)
    t = seg(C_IKW, C_BQ)
    ik_ref[...] = t[:, :IDX_DIM]
    iw_ref[...] = t[:, IDX_DIM:IDX_DIM + N_HEADS] * IDX_SCALE

    rc, rsa, rsb = rc_ref[...], rsa_ref[...], rsb_ref[...]

    def rot(r):
        return r * rc + pltpu.roll(r, 16, 1) * rsa + pltpu.roll(r, 112, 1) * rsb

    nq = _rms(seg(C_BQ, C_BKV), gq_ref[...]).astype(BF16)
    for h in range(N_HEADS):
        z = _dot(nq, wq2_ref[:, h * 256:(h + 1) * 256])
        qb_ref[:, h * 256:h * 256 + 128] = (z[:, :128] * MLA_SCALE).astype(BF16)
        qb_ref[:, h * 256 + 128:(h + 1) * 256] = (rot(z[:, 128:]) * MLA_SCALE).astype(BF16)
    c = _rms(seg(C_BKV, C_BKR), gkv_ref[...])
    ckv_ref[...] = c
    kr = rot(seg(C_BKR, C_CQ))
    kr_ref[...] = kr[:, :ROPE_DIM]
    kb_ref[:, :128] = c.astype(BF16)
    kb_ref[:, 128:] = kr.astype(BF16)
    cq_ref[...] = (seg(C_CQ, C_CK) * HEAD_DIM ** -0.5).astype(BF16)
    ck_ref[...] = seg(C_CK, C_CV)
    cv_ref[...] = seg(C_CV, C_GATE)
    gate_ref[...] = jax.nn.sigmoid(seg(C_GATE, C_END)).astype(BF16)


def _project(h, gmix, w_main, gq, wq2, gkv, rope_tabs, n_prompt_blocks, rope_period, tm):
    n_tok = h.shape[0]
    rc, rsa, rsb = rope_tabs

    def rope_map(i):
        return (jnp.where(i < n_prompt_blocks, i % rope_period, rope_period), 0)

    row = lambda w: pl.BlockSpec((tm, w), lambda i: (i, 0))
    full = lambda a: pl.BlockSpec(a.shape, lambda i: (0,) * a.ndim)
    widths = [(512, BF16), (64, F32), (64, F32), (512, BF16), (64, F32), (8, F32), (2048, BF16),
              (128, F32), (32, F32), (256, BF16), (512, BF16), (128, F32), (128, F32), (3072, BF16)]
    return pl.pallas_call(
        _project_kernel,
        grid=(n_tok // tm,),
        in_specs=[row(D_MODEL), full(gmix), full(w_main), full(gq), full(wq2), full(gkv),
                  pl.BlockSpec((tm, 128), rope_map), pl.BlockSpec((tm, 128), rope_map),
                  pl.BlockSpec((tm, 128), rope_map)],
        out_specs=[row(w) for w, _ in widths],
        out_shape=[jax.ShapeDtypeStruct((n_tok, w), d) for w, d in widths],
        compiler_params=_cparams(("parallel",)),
        name="project",
    )(h, gmix, w_main, gq, wq2, gkv, rc, rsa, rsb)


def _sortable(x):
    b = pltpu.bitcast(x, I32)
    k = jnp.where(b >= 0, b, b ^ 0x7FFFFFFF)
    return jnp.where(b == INT_MIN, 0, k)


def _topk_mask(key_ref, p_ref, k_sel, idx, idx_bits, count):
    kf = float(k_sel)
    zero = count(key_ref[...] >= 0)
    t0 = jnp.where(zero >= kf, 0, INT_MIN).astype(I32)

    def bis(i, t):
        cand = t + jnp.left_shift(jnp.int32(1), 30 - i)
        return jnp.where(count(key_ref[...] >= cand) >= kf, cand, t)

    thr = lax.fori_loop(0, 31, bis, t0)
    key = key_ref[...]
    gt = key > thr
    eq = key == thr
    need = kf - count(gt)
    excess = jnp.logical_and(count(eq) > need, thr != INT_MIN)
    p_ref[...] = jnp.full(p_ref.shape, 2 ** idx_bits, I32)

    @pl.when(jnp.max(excess.astype(F32)) > 0.0)
    def _():
        def bis_idx(i, p):
            cand = p + jnp.left_shift(jnp.int32(1), idx_bits - 1 - i)
            c = count(jnp.logical_and(eq, idx < cand))
            return jnp.where(c < need, cand, p)

        p = lax.fori_loop(0, idx_bits, bis_idx, jnp.zeros(p_ref.shape, I32))
        p_ref[...] = jnp.where(excess, p, 2 ** idx_bits)

    return jnp.logical_or(gt, jnp.logical_and(eq, idx <= p_ref[...]))


def _dsa_select_kernel(iq_ref, iw_ref, ik_ref, mask_ref, key_scr, p_scr, *, k_sel, idx_bits):
    tq, s_len = mask_ref.shape
    qi = pl.program_id(1)
    ik = ik_ref[...].astype(BF16)
    iq = iq_ref[...]
    iw = iw_ref[...]
    sc = jnp.zeros((tq, s_len), F32)
    for h in range(N_HEADS):
        s = _dot_nt(iq[:, h * IDX_DIM:(h + 1) * IDX_DIM], ik)
        sc = sc + iw[:, h:h + 1] * jnp.maximum(s, 0.0)
    row = qi * tq + lax.broadcasted_iota(I32, (tq, s_len), 0)
    col = lax.broadcasted_iota(I32, (tq, s_len), 1)
    causal = col <= row
    key_scr[...] = jnp.where(causal, _sortable(sc), INT_MIN)

    def count(m):
        return jnp.sum(m.astype(F32), axis=1, keepdims=True)

    sel = _topk_mask(key_scr, p_scr, k_sel, col, idx_bits, count)
    mask_ref[...] = jnp.where(jnp.logical_and(sel, causal), 1.0, 0.0).astype(BF16)


def _dsa_select(iq, iw, ik, batch, s_len, tq):
    nq = s_len // tq
    k_sel = min(TOPK_MAX, s_len // 4)
    idx_bits = max(1, (s_len - 1).bit_length())
    return pl.pallas_call(
        functools.partial(_dsa_select_kernel, k_sel=k_sel, idx_bits=idx_bits),
        grid=(batch, nq),
        in_specs=[pl.BlockSpec((tq, 512), lambda b, q: (b * nq + q, 0)),
                  pl.BlockSpec((tq, N_HEADS), lambda b, q: (b * nq + q, 0)),
                  pl.BlockSpec((s_len, IDX_DIM), lambda b, q: (b, 0))],
        out_specs=pl.BlockSpec((tq, s_len), lambda b, q: (b * nq + q, 0)),
        out_shape=jax.ShapeDtypeStruct((batch * s_len, s_len), BF16),
        scratch_shapes=[pltpu.VMEM((tq, s_len), I32), pltpu.VMEM((tq, 1), I32)],
        compiler_params=_cparams(("parallel", "parallel")),
        name="dsa_select",
    )(iq, iw, ik)


def _softmax_step(s, h, m_scr, l_scr, acc_scr, v):
    m_old = m_scr[h]
    m_new = jnp.maximum(m_old, jnp.max(s, axis=1, keepdims=True))
    alpha = jnp.exp(m_old - m_new)
    p = jnp.exp(s - m_new)
    l_scr[h] = alpha * l_scr[h] + jnp.sum(p, axis=1, keepdims=True)
    acc_scr[h] = alpha * acc_scr[h] + _dot(p.astype(BF16), v)
    m_scr[h] = m_new


def _dsa_attn_kernel(q_ref, k_ref, v_ref, mask_ref, bias_ref, o_ref, qh_scr, m_scr, l_scr, acc_scr):
    qi, ki = pl.program_id(1), pl.program_id(2)
    nk = pl.num_programs(2)

    @pl.when(ki == 0)
    def _():
        m_scr[...] = jnp.full(m_scr.shape, NEG, F32)
        l_scr[...] = jnp.zeros(l_scr.shape, F32)
        acc_scr[...] = jnp.zeros(acc_scr.shape, F32)
        for h in range(N_HEADS):
            qh_scr[h] = q_ref[:, h * HEAD_DIM:(h + 1) * HEAD_DIM]

    @pl.when(ki <= qi)
    def _():
        k = k_ref[...].astype(BF16)
        v = v_ref[...].astype(BF16)
        msk = mask_ref[...] > 0
        tix = jnp.minimum(qi - ki, 2)
        for h in range(N_HEADS):
            s = _dot_nt(qh_scr[h], k) + bias_ref[tix, h]
            _softmax_step(jnp.where(msk, s, NEG), h, m_scr, l_scr, acc_scr, v)

    @pl.when(ki == nk - 1)
    def _():
        for h in range(N_HEADS):
            o_ref[:, h * HEAD_DIM:(h + 1) * HEAD_DIM] = (acc_scr[h] / l_scr[h]).astype(BF16)


def _dsa_attn(aq, ak, av, mask, bias_tiles, batch, s_len, t):
    nb = s_len // t
    return pl.pallas_call(
        _dsa_attn_kernel,
        grid=(batch, nb, nb),
        in_specs=[pl.BlockSpec((t, 512), lambda b, q, k: (b * nb + q, 0)),
                  pl.BlockSpec((t, HEAD_DIM), lambda b, q, k: (b * nb + jnp.minimum(k, q), 0)),
                  pl.BlockSpec((t, HEAD_DIM), lambda b, q, k: (b * nb + jnp.minimum(k, q), 0)),
                  pl.BlockSpec((t, t), lambda b, q, k: (b * nb + q, jnp.minimum(k, q))),
                  pl.BlockSpec(bias_tiles.shape, lambda b, q, k: (0, 0, 0, 0))],
        out_specs=pl.BlockSpec((t, 512), lambda b, q, k: (b * nb + q, 0)),
        out_shape=jax.ShapeDtypeStruct((batch * s_len, 512), BF16),
        scratch_shapes=[pltpu.VMEM((N_HEADS, t, HEAD_DIM), BF16), pltpu.VMEM((N_HEADS, t, 1), F32),
                        pltpu.VMEM((N_HEADS, t, 1), F32), pltpu.VMEM((N_HEADS, t, HEAD_DIM), F32)],
        compiler_params=_cparams(("parallel", "parallel", "arbitrary")),
        name="dsa_attn",
    )(aq, ak, av, mask, bias_tiles)


def _mla_attn_kernel(q_ref, kb_ref, o_ref, m_scr, l_scr, acc_scr):
    qi, ki = pl.program_id(1), pl.program_id(2)
    nk = pl.num_programs(2)
    t = kb_ref.shape[0]

    @pl.when(ki == 0)
    def _():
        m_scr[...] = jnp.full(m_scr.shape, NEG, F32)
        l_scr[...] = jnp.zeros(l_scr.shape, F32)
        acc_scr[...] = jnp.zeros(acc_scr.shape, F32)

    @pl.when(ki <= qi)
    def _():
        kb = kb_ref[...]
        v = kb[:, :KV_LORA]
        row = qi * t + lax.broadcasted_iota(I32, (t, t), 0)
        col = ki * t + lax.broadcasted_iota(I32, (t, t), 1)
        msk = col <= row
        for h in range(N_HEADS):
            s = _dot_nt(q_ref[:, h * 256:(h + 1) * 256], kb)
            _softmax_step(jnp.where(msk, s, NEG), h, m_scr, l_scr, acc_scr, v)

    @pl.when(ki == nk - 1)
    def _():
        for h in range(N_HEADS):
            o_ref[:, h * KV_LORA:(h + 1) * KV_LORA] = (acc_scr[h] / l_scr[h]).astype(BF16)


def _mla_attn(qb, kb, batch, s_len, t):
    nb = s_len // t
    return pl.pallas_call(
        _mla_attn_kernel,
        grid=(batch, nb, nb),
        in_specs=[pl.BlockSpec((t, 2048), lambda b, q, k: (b * nb + q, 0)),
                  pl.BlockSpec((t, 256), lambda b, q, k: (b * nb + jnp.minimum(k, q), 0))],
        out_specs=pl.BlockSpec((t, 1024), lambda b, q, k: (b * nb + q, 0)),
        out_shape=jax.ShapeDtypeStruct((batch * s_len, 1024), BF16),
        scratch_shapes=[pltpu.VMEM((N_HEADS, t, 1), F32), pltpu.VMEM((N_HEADS, t, 1), F32),
                        pltpu.VMEM((N_HEADS, t, KV_LORA), F32)],
        compiler_params=_cparams(("parallel", "parallel", "arbitrary")),
        name="mla_attn",
    )(qb, kb)


def _sb_block(z, strict, tri, carry, v):
    lk = -(jnp.maximum(z, 0.0) + jnp.log(1.0 + jnp.exp(-jnp.abs(z))))
    if strict is not None:
        lk = jnp.where(strict, lk, 0.0)
    hi = lk.astype(BF16)
    lo = (lk - hi.astype(F32)).astype(BF16)
    cs = _dot(hi, tri) + _dot(lo, tri)
    a = jnp.exp(z + cs + carry)
    if strict is not None:
        a = jnp.where(strict, a, 0.0)
    return _dot(a.astype(BF16), v), carry + cs[:, 0:1]


def _tri(n):
    return jnp.where(lax.broadcasted_iota(I32, (n, n), 0) >= lax.broadcasted_iota(I32, (n, n), 1),
                     1.0, 0.0).astype(BF16)


def _sb_attn_kernel(q_ref, k_ref, v_ref, o_ref, qh_scr, carry_scr, acc_scr):
    qi, j = pl.program_id(1), pl.program_id(2)
    nk = pl.num_programs(2)
    t = k_ref.shape[0]
    ki = qi - j

    @pl.when(j == 0)
    def _():
        carry_scr[...] = jnp.zeros(carry_scr.shape, F32)
        acc_scr[...] = jnp.zeros(acc_scr.shape, F32)
        for h in range(N_HEADS):
            qh_scr[h] = q_ref[:, h * HEAD_DIM:(h + 1) * HEAD_DIM]

    @pl.when(j <= qi)
    def _():
        k = k_ref[...].astype(BF16)
        v = v_ref[...].astype(BF16)
        row = qi * t + lax.broadcasted_iota(I32, (t, t), 0)
        col = ki * t + lax.broadcasted_iota(I32, (t, t), 1)
        strict = col < row
        tri = _tri(t)
        for h in range(N_HEADS):
            n = h // C_GROUP
            z = _dot_nt(qh_scr[h], k[:, n * HEAD_DIM:(n + 1) * HEAD_DIM])
            o, c = _sb_block(z, strict, tri, carry_scr[h], v[:, n * HEAD_DIM:(n + 1) * HEAD_DIM])
            acc_scr[h] = acc_scr[h] + o
            carry_scr[h] = c

    @pl.when(j == nk - 1)
    def _():
        for h in range(N_HEADS):
            o_ref[:, h * HEAD_DIM:(h + 1) * HEAD_DIM] = acc_scr[h].astype(BF16)


def _sb_attn(cq, ck, cv, batch, s_len, t):
    nb = s_len // t
    kmap = lambda b, q, j: (b * nb + jnp.maximum(q - j, 0), 0)
    return pl.pallas_call(
        _sb_attn_kernel,
        grid=(batch, nb, nb),
        in_specs=[pl.BlockSpec((t, 512), lambda b, q, j: (b * nb + q, 0)),
                  pl.BlockSpec((t, 128), kmap), pl.BlockSpec((t, 128), kmap)],
        out_specs=pl.BlockSpec((t, 512), lambda b, q, j: (b * nb + q, 0)),
        out_shape=jax.ShapeDtypeStruct((batch * s_len, 512), BF16),
        scratch_shapes=[pltpu.VMEM((N_HEADS, t, HEAD_DIM), BF16), pltpu.VMEM((N_HEADS, t, 1), F32),
                        pltpu.VMEM((N_HEADS, t, HEAD_DIM), F32)],
        compiler_params=_cparams(("parallel", "parallel", "arbitrary")),
        name="sb_attn",
    )(cq, ck, cv)


def _page_specs(cache, layer, g_pages, n_pages, page_of):
    page, width = cache.shape[2], cache.shape[3]

    def spec(g):
        return pl.BlockSpec((None, None, page, width),
                            lambda b, j, pt: (layer, pt[b * n_pages + page_of(j, g)], 0, 0))

    return [spec(g) for g in range(g_pages)]


def _samp_select_kernel(pt_ref, iq_ref, iw_ref, iknew_ref, *rest, g_pages, n_pages, k_sel, idx_bits):
    pages = rest[:g_pages]
    out_ref, sc_scr, key_scr, p_scr = rest[g_pages:]
    j = pl.program_id(1)
    ns = pl.num_programs(1)
    q = iq_ref[...]
    w = iw_ref[...]
    n_tok = out_ref.shape[1]

    def score(kblk):
        s = jnp.maximum(_dot_nt(q, kblk.astype(BF16)), 0.0) * w
        tot = s[0:n_tok]
        for h in range(1, N_HEADS):
            tot = tot + s[h * n_tok:(h + 1) * n_tok]
        return tot

    for g in range(g_pages):
        sc_scr[j * g_pages + g] = score(pages[g][...])

    @pl.when(j == ns - 1)
    def _():
        sc_scr[n_pages] = score(iknew_ref[...])
        shape = sc_scr.shape
        page_i = lax.broadcasted_iota(I32, shape, 0)
        row = lax.broadcasted_iota(I32, shape, 1)
        lane = lax.broadcasted_iota(I32, shape, 2)
        causal = jnp.logical_or(page_i < n_pages, lane <= row)
        key_scr[...] = jnp.where(causal, _sortable(sc_scr[...]), INT_MIN)

        def count(m):
            return jnp.sum(jnp.sum(m.astype(F32), axis=0), axis=1, keepdims=True)[None]

        sel = _topk_mask(key_scr, p_scr, k_sel, page_i * 128 + lane, idx_bits, count)
        out_ref[...] = jnp.where(jnp.logical_and(sel, causal), 0.0, NEG)


def _samp_select(page_table, iq_s, iw_s, ik_new, cache_idx, layer, g_pages):
    n_seq, n_tok = ik_new.shape[0], iq_s.shape[1] // N_HEADS
    n_pages = page_table.shape[1]
    ns = n_pages // g_pages
    n_keys = n_pages * 128 + n_tok
    k_sel = min(TOPK_MAX, n_keys // 4)
    idx_bits = ((n_pages + 1) * 128 - 1).bit_length()
    grid_spec = pltpu.PrefetchScalarGridSpec(
        num_scalar_prefetch=1,
        grid=(n_seq, ns),
        in_specs=[pl.BlockSpec((None, N_HEADS * n_tok, IDX_DIM), lambda b, j, pt: (b, 0, 0)),
                  pl.BlockSpec((None, N_HEADS * n_tok, 1), lambda b, j, pt: (b, 0, 0)),
                  pl.BlockSpec((None, 128, IDX_DIM), lambda b, j, pt: (b, 0, 0))]
        + _page_specs(cache_idx, layer, g_pages, n_pages, lambda j, g: j * g_pages + g),
        out_specs=pl.BlockSpec((None, n_pages + 1, n_tok, 128), lambda b, j, pt: (b, 0, 0, 0)),
        scratch_shapes=[pltpu.VMEM((n_pages + 1, n_tok, 128), F32),
                        pltpu.VMEM((n_pages + 1, n_tok, 128), I32),
                        pltpu.VMEM((1, n_tok, 1), I32)],
    )
    return pl.pallas_call(
        functools.partial(_samp_select_kernel, g_pages=g_pages, n_pages=n_pages, k_sel=k_sel,
                          idx_bits=idx_bits),
        grid_spec=grid_spec,
        out_shape=jax.ShapeDtypeStruct((n_seq, n_pages + 1, n_tok, 128), F32),
        compiler_params=_cparams(("parallel", "arbitrary")),
        name="samp_select",
    )(page_table.reshape(-1), iq_s, iw_s, ik_new, *([cache_idx] * g_pages))


def _softmax_step2(s, m_scr, l_scr, acc_scr, v):
    m_old = m_scr[...]
    m_new = jnp.maximum(m_old, jnp.max(s, axis=1, keepdims=True))
    alpha = jnp.exp(m_old - m_new)
    p = jnp.exp(s - m_new)
    l_scr[...] = alpha * l_scr[...] + jnp.sum(p, axis=1, keepdims=True)
    acc_scr[...] = alpha * acc_scr[...] + _dot(p.astype(BF16), v)
    m_scr[...] = m_new


def _samp_dsa_kernel(pt_ref, q_ref, am_ref, bias_ref, knew_ref, vnew_ref, *rest, g_pages, n_pages):
    kp = rest[:g_pages]
    vp = rest[g_pages:2 * g_pages]
    o_ref, m_scr, l_scr, acc_scr = rest[2 * g_pages:]
    j = pl.program_id(1)
    ns = pl.num_programs(1)
    q = q_ref[...]

    @pl.when(j == 0)
    def _():
        m_scr[...] = jnp.full(m_scr.shape, NEG, F32)
        l_scr[...] = jnp.zeros(l_scr.shape, F32)
        acc_scr[...] = jnp.zeros(acc_scr.shape, F32)

    def step(k, v, page):
        am = am_ref[page]
        s = _dot_nt(q, k.astype(BF16)) + bias_ref[page] + jnp.concatenate([am] * N_HEADS, axis=0)
        _softmax_step2(s, m_scr, l_scr, acc_scr, v.astype(BF16))

    for g in range(g_pages):
        step(kp[g][...], vp[g][...], j * g_pages + g)

    @pl.when(j == ns - 1)
    def _():
        step(knew_ref[...], vnew_ref[...], n_pages)
        o_ref[...] = acc_scr[...] / l_scr[...]


def _samp_dsa(page_table, q_s, addmask, bias_s, k_new, v_new, cache_k, cache_v, layer, g_pages):
    n_seq, rows = q_s.shape[0], q_s.shape[1]
    n_pages = page_table.shape[1]
    n_tok = rows // N_HEADS
    page_of = lambda j, g: j * g_pages + g
    grid_spec = pltpu.PrefetchScalarGridSpec(
        num_scalar_prefetch=1,
        grid=(n_seq, n_pages // g_pages),
        in_specs=[pl.BlockSpec((None, rows, HEAD_DIM), lambda b, j, pt: (b, 0, 0)),
                  pl.BlockSpec((None, n_pages + 1, n_tok, 128), lambda b, j, pt: (b, 0, 0, 0)),
                  pl.BlockSpec(bias_s.shape, lambda b, j, pt: (0, 0, 0)),
                  pl.BlockSpec((None, 128, HEAD_DIM), lambda b, j, pt: (b, 0, 0)),
                  pl.BlockSpec((None, 128, HEAD_DIM), lambda b, j, pt: (b, 0, 0))]
        + _page_specs(cache_k, layer, g_pages, n_pages, page_of)
        + _page_specs(cache_v, layer, g_pages, n_pages, page_of),
        out_specs=pl.BlockSpec((None, rows, HEAD_DIM), lambda b, j, pt: (b, 0, 0)),
        scratch_shapes=[pltpu.VMEM((rows, 1), F32), pltpu.VMEM((rows, 1), F32),
                        pltpu.VMEM((rows, HEAD_DIM), F32)],
    )
    return pl.pallas_call(
        functools.partial(_samp_dsa_kernel, g_pages=g_pages, n_pages=n_pages),
        grid_spec=grid_spec,
        out_shape=jax.ShapeDtypeStruct((n_seq, rows, HEAD_DIM), F32),
        compiler_params=_cparams(("parallel", "arbitrary")),
        name="samp_dsa",
    )(page_table.reshape(-1), q_s, addmask, bias_s, k_new, v_new,
      *([cache_k] * g_pages), *([cache_v] * g_pages))


def _samp_mla_kernel(pt_ref, q_ref, kbnew_ref, *rest, g_pages, n_tok):
    cp = rest[:g_pages]
    rp = rest[g_pages:2 * g_pages]
    o_ref, m_scr, l_scr, acc_scr = rest[2 * g_pages:]
    j = pl.program_id(1)
    ns = pl.num_programs(1)
    q = q_ref[...]

    @pl.when(j == 0)
    def _():
        m_scr[...] = jnp.full(m_scr.shape, NEG, F32)
        l_scr[...] = jnp.zeros(l_scr.shape, F32)
        acc_scr[...] = jnp.zeros(acc_scr.shape, F32)

    for g in range(g_pages):
        c = cp[g][...].astype(BF16)
        s = _dot_nt(q[:, :KV_LORA], c) + _dot_nt(q[:, KV_LORA:KV_LORA + ROPE_DIM], rp[g][...].astype(BF16))
        _softmax_step2(s, m_scr, l_scr, acc_scr, c)

    @pl.when(j == ns - 1)
    def _():
        kb = kbnew_ref[...]
        s = _dot_nt(q, kb)
        row = lax.broadcasted_iota(I32, s.shape, 0) % n_tok
        col = lax.broadcasted_iota(I32, s.shape, 1)
        _softmax_step2(jnp.where(col <= row, s, NEG), m_scr, l_scr, acc_scr, kb[:, :KV_LORA])
        o_ref[...] = acc_scr[...] / l_scr[...]


def _samp_mla(page_table, q_s, kb_new, cache_ckv, cache_kr, layer, g_pages):
    n_seq, rows = q_s.shape[0], q_s.shape[1]
    n_pages = page_table.shape[1]
    page_of = lambda j, g: j * g_pages + g
    grid_spec = pltpu.PrefetchScalarGridSpec(
        num_scalar_prefetch=1,
        grid=(n_seq, n_pages // g_pages),
        in_specs=[pl.BlockSpec((None, rows, 256), lambda b, j, pt: (b, 0, 0)),
                  pl.BlockSpec((None, 128, 256), lambda b, j, pt: (b, 0, 0))]
        + _page_specs(cache_ckv, layer, g_pages, n_pages, page_of)
        + _page_specs(cache_kr, layer, g_pages, n_pages, page_of),
        out_specs=pl.BlockSpec((None, rows, KV_LORA), lambda b, j, pt: (b, 0, 0)),
        scratch_shapes=[pltpu.VMEM((rows, 1), F32), pltpu.VMEM((rows, 1), F32),
                        pltpu.VMEM((rows, KV_LORA), F32)],
    )
    return pl.pallas_call(
        functools.partial(_samp_mla_kernel, g_pages=g_pages, n_tok=rows // N_HEADS),
        grid_spec=grid_spec,
        out_shape=jax.ShapeDtypeStruct((n_seq, rows, KV_LORA), F32),
        compiler_params=_cparams(("parallel", "arbitrary")),
        name="samp_mla",
    )(page_table.reshape(-1), q_s, kb_new, *([cache_ckv] * g_pages), *([cache_kr] * g_pages))


def _samp_sb_kernel(pt_ref, q_ref, knew_ref, vnew_ref, *rest, g_pages, n_tok):
    kp = rest[:g_pages]
    vp = rest[g_pages:2 * g_pages]
    o_ref, carry_scr, acc_scr = rest[2 * g_pages:]
    j = pl.program_id(1)
    ns = pl.num_programs(1)
    rows = q_ref.shape[1]
    tri = _tri(128)

    def block(k, v, strict):
        k = k.astype(BF16)
        v = v.astype(BF16)
        for n in range(C_KV_HEADS):
            sl = slice(n * HEAD_DIM, (n + 1) * HEAD_DIM)
            z = _dot_nt(q_ref[n], k[:, sl])
            o, c = _sb_block(z, strict, tri, carry_scr[n], v[:, sl])
            acc_scr[n] = acc_scr[n] + o
            carry_scr[n] = c

    @pl.when(j == 0)
    def _():
        carry_scr[...] = jnp.zeros(carry_scr.shape, F32)
        acc_scr[...] = jnp.zeros(acc_scr.shape, F32)
        row = lax.broadcasted_iota(I32, (rows, 128), 0) % n_tok
        col = lax.broadcasted_iota(I32, (rows, 128), 1)
        block(knew_ref[...], vnew_ref[...], col < row)

    for g in reversed(range(g_pages)):
        block(kp[g][...], vp[g][...], None)

    @pl.when(j == ns - 1)
    def _():
        o_ref[...] = acc_scr[...]


def _samp_sb(page_table, q_s, k_new, v_new, cache_k, cache_v, layer, g_pages, n_tok):
    n_seq, rows = q_s.shape[0], q_s.shape[2]
    n_pages = page_table.shape[1]
    ns = n_pages // g_pages
    page_of = lambda j, g: (ns - 1 - j) * g_pages + g
    grid_spec = pltpu.PrefetchScalarGridSpec(
        num_scalar_prefetch=1,
        grid=(n_seq, ns),
        in_specs=[pl.BlockSpec((None, C_KV_HEADS, rows, HEAD_DIM), lambda b, j, pt: (b, 0, 0, 0)),
                  pl.BlockSpec((None, 128, 128), lambda b, j, pt: (b, 0, 0)),
                  pl.BlockSpec((None, 128, 128), lambda b, j, pt: (b, 0, 0))]
        + _page_specs(cache_k, layer, g_pages, n_pages, page_of)
        + _page_specs(cache_v, layer, g_pages, n_pages, page_of),
        out_specs=pl.BlockSpec((None, C_KV_HEADS, rows, HEAD_DIM), lambda b, j, pt: (b, 0, 0, 0)),
        scratch_shapes=[pltpu.VMEM((C_KV_HEADS, rows, 1), F32),
                        pltpu.VMEM((C_KV_HEADS, rows, HEAD_DIM), F32)],
    )
    return pl.pallas_call(
        functools.partial(_samp_sb_kernel, g_pages=g_pages, n_tok=n_tok),
        grid_spec=grid_spec,
        out_shape=jax.ShapeDtypeStruct((n_seq, C_KV_HEADS, rows, HEAD_DIM), F32),
        compiler_params=_cparams(("parallel", "arbitrary")),
        name="samp_sb",
    )(page_table.reshape(-1), q_s, k_new, v_new, *([cache_k] * g_pages), *([cache_v] * g_pages))


def _merge_kernel(h_ref, oa_ref, ob_ref, oc_ref, gate_ref, wa_ref, wb_ref, wc_ref, wo_ref, gffn_ref,
                  wpq_ref, sk1_ref, sk2_ref, h1_ref, n2_ref, s_ref):
    g = gate_ref[...].astype(F32)
    m = (g[:, :D_MODEL] * _dot(oa_ref[...], wa_ref[...])
         + g[:, D_MODEL:2 * D_MODEL] * _dot(ob_ref[...], wb_ref[...])
         + g[:, 2 * D_MODEL:] * _dot(oc_ref[...], wc_ref[...]))
    h1 = h_ref[...] + _dot(m.astype(BF16), wo_ref[...])
    h1_ref[...] = h1
    n2 = _rms(h1, gffn_ref[...])
    n2_ref[...] = n2
    q = _dot(n2.astype(BF16), wpq_ref[...])
    for h in range(N_HEADS):
        for j, sk in enumerate((sk1_ref, sk2_ref)):
            qh = q[:, h * 256 + j * 128:h * 256 + (j + 1) * 128].astype(BF16)
            s_ref[j, h] = _dot_nt(sk[...], qh)


def _merge(h, oa, ob, oc, gates, wa, wb, wc, wo, gffn, wpq, sk1, sk2, tm):
    n_tok = h.shape[0]
    row = lambda w: pl.BlockSpec((tm, w), lambda i: (i, 0))
    full = lambda a: pl.BlockSpec(a.shape, lambda i: (0,) * a.ndim)
    return pl.pallas_call(
        _merge_kernel,
        grid=(n_tok // tm,),
        in_specs=[row(D_MODEL), row(512), row(1024), row(512), row(3072), full(wa), full(wb), full(wc),
                  full(wo), full(gffn), full(wpq), full(sk1), full(sk2)],
        out_specs=[row(D_MODEL), row(D_MODEL),
                   pl.BlockSpec((2, N_HEADS, N_KEYS, tm), lambda i: (0, 0, 0, i))],
        out_shape=[jax.ShapeDtypeStruct((n_tok, D_MODEL), F32), jax.ShapeDtypeStruct((n_tok, D_MODEL), F32),
                   jax.ShapeDtypeStruct((2, N_HEADS, N_KEYS, n_tok), F32)],
        compiler_params=_cparams(("parallel",)),
        name="merge",
    )(h, oa, ob, oc, gates, wa, wb, wc, wo, gffn, wpq, sk1, sk2)


def _extract16(s, ids):
    vals, idxs = [], []
    for _ in range(PEER_TOPK):
        m = jnp.max(s, axis=0, keepdims=True)
        idx = jnp.min(jnp.where(s == m, ids, 2 ** 30), axis=0, keepdims=True)
        vals.append(m)
        idxs.append(idx)
        s = jnp.where(ids == idx, -jnp.inf, s)
    return vals, idxs


def _peer_topk_kernel(s_ref, e_ref, g_ref):
    cols = s_ref.shape[-1]
    kio = lax.broadcasted_iota(I32, (N_KEYS, 128), 0)
    for c in range(cols // 128):
        sl = slice(c * 128, (c + 1) * 128)
        v1, i1 = _extract16(s_ref[0, 0, :, sl], kio)
        v2, i2 = _extract16(s_ref[1, 0, :, sl], kio)
        v1a, i1a = jnp.concatenate(v1, axis=0), jnp.concatenate(i1, axis=0)
        v2a, i2a = jnp.concatenate(v2, axis=0), jnp.concatenate(i2, axis=0)
        cv = [v1[0] + v2a] + [v1[a] + v2a[:8] for a in range(1, 8)] + [v1a[8:] + v2[0]]
        ci = [i1[0] * N_KEYS + i2a] + [i1[a] * N_KEYS + i2a[:8] for a in range(1, 8)] + [i1a[8:] * N_KEYS + i2[0]]
        sc, e = _extract16(jnp.concatenate(cv, axis=0), jnp.concatenate(ci, axis=0))
        sc = jnp.concatenate(sc, axis=0)
        p = jnp.exp(sc - sc[0:1])
        e_ref[:, sl] = jnp.concatenate(e, axis=0)
        g_ref[:, sl] = p / jnp.sum(p, axis=0, keepdims=True)


def _peer_topk(s, tc):
    n_tok = s.shape[-1]
    return pl.pallas_call(
        _peer_topk_kernel,
        grid=(N_HEADS, n_tok // tc),
        in_specs=[pl.BlockSpec((2, 1, N_KEYS, tc), lambda h, i: (0, h, 0, i))],
        out_specs=[pl.BlockSpec((PEER_TOPK, tc), lambda h, i: (h, i)),
                   pl.BlockSpec((PEER_TOPK, tc), lambda h, i: (h, i))],
        out_shape=[jax.ShapeDtypeStruct((PEER_PAIRS, n_tok), I32),
                   jax.ShapeDtypeStruct((PEER_PAIRS, n_tok), F32)],
        compiler_params=_cparams(("parallel", "parallel")),
        name="peer_topk",
    )(s)


def _unpack(w):
    hi = pltpu.bitcast(jnp.bitwise_and(w, jnp.uint32(0xFFFF0000)), F32)
    lo = pltpu.bitcast(jnp.left_shift(w, jnp.uint32(16)), F32)
    return hi, lo


def _peer_u_kernel(e_ref, x_ref, g_ref, tab_ref, w_ref, p_scr):
    tb = x_ref.shape[0]
    lane = lax.broadcasted_iota(I32, (PEER_PAIRS, tb), 1)

    def tok(t, acts):
        x = x_ref[t]
        xh, xl = x[0:4], x[4:8]
        for j in range(PEER_PAIRS):
            hi, lo = _unpack(tab_ref[e_ref[j, t]])
            p_scr[4 * j:4 * j + 4, :] = hi * xh + lo * xl
        r = (p_scr[pl.ds(0, PEER_PAIRS, stride=4), :] + p_scr[pl.ds(1, PEER_PAIRS, stride=4), :]
             + p_scr[pl.ds(2, PEER_PAIRS, stride=4), :] + p_scr[pl.ds(3, PEER_PAIRS, stride=4), :])
        return jnp.where(lane == t, jnp.sum(r, axis=1, keepdims=True), acts)

    a = lax.fori_loop(0, tb, tok, jnp.zeros((PEER_PAIRS, tb), F32))
    w_ref[...] = g_ref[...] * jax.nn.gelu(a)


def _peer_u(e_t, x3, g_t, tab, tb):
    n_tok = x3.shape[0]
    return pl.pallas_call(
        _peer_u_kernel,
        grid=(n_tok // tb,),
        in_specs=[pl.BlockSpec((PEER_PAIRS, tb), lambda i: (0, i), memory_space=pltpu.SMEM),
                  pl.BlockSpec((tb, 8, 128), lambda i: (i, 0, 0)),
                  pl.BlockSpec((PEER_PAIRS, tb), lambda i: (0, i)),
                  pl.BlockSpec(memory_space=pltpu.VMEM)],
        out_specs=pl.BlockSpec((PEER_PAIRS, tb), lambda i: (0, i)),
        out_shape=jax.ShapeDtypeStruct((PEER_PAIRS, n_tok), F32),
        scratch_shapes=[pltpu.VMEM((4 * PEER_PAIRS, 128), F32)],
        compiler_params=_cparams(("arbitrary",)),
        name="peer_u",
    )(e_t, x3, g_t, tab)


def _peer_v_kernel(e_ref, w_ref, tab_ref, o_ref):
    tb = o_ref.shape[0]

    def tok(t, carry):
        acc_h = [jnp.zeros((4, 128), F32) for _ in range(4)]
        acc_l = [jnp.zeros((4, 128), F32) for _ in range(4)]
        for j in range(PEER_PAIRS):
            hi, lo = _unpack(tab_ref[e_ref[j, t]])
            wgt = w_ref[j, t]
            acc_h[j % 4] = acc_h[j % 4] + wgt * hi
            acc_l[j % 4] = acc_l[j % 4] + wgt * lo
        o_ref[t] = jnp.concatenate([(acc_h[0] + acc_h[1]) + (acc_h[2] + acc_h[3]),
                                    (acc_l[0] + acc_l[1]) + (acc_l[2] + acc_l[3])], axis=0)
        return carry

    lax.fori_loop(0, tb, tok, 0)


def _peer_v(e_t, w_t, tab, tb):
    n_tok = e_t.shape[1]
    return pl.pallas_call(
        _peer_v_kernel,
        grid=(n_tok // tb,),
        in_specs=[pl.BlockSpec((PEER_PAIRS, tb), lambda i: (0, i), memory_space=pltpu.SMEM),
                  pl.BlockSpec((PEER_PAIRS, tb), lambda i: (0, i), memory_space=pltpu.SMEM),
                  pl.BlockSpec(memory_space=pltpu.VMEM)],
        out_specs=pl.BlockSpec((tb, 8, 128), lambda i: (i, 0, 0)),
        out_shape=jax.ShapeDtypeStruct((n_tok, 8, 128), F32),
        compiler_params=_cparams(("arbitrary",)),
        name="peer_v",
    )(e_t, w_t, tab)


def _ple_kernel(h1_ref, po_ref, p_ref, gple_ref, wpg_ref, wpp_ref, gfin_ref, h_ref, y_ref):
    h2 = h1_ref[...] + po_ref[...]
    n = _rms(h2, gple_ref[...]).astype(BF16)
    h3 = h2 + jax.nn.sigmoid(_dot(n, wpg_ref[...])) * _dot(p_ref[...].astype(BF16), wpp_ref[...])
    h_ref[...] = h3
    y_ref[...] = _rms(h3, gfin_ref[...])


def _ple(h1, po, p, gple, wpg, wpp, gfin, tm):
    n_tok = h1.shape[0]
    row = lambda w: pl.BlockSpec((tm, w), lambda i: (i, 0))
    full = lambda a: pl.BlockSpec(a.shape, lambda i: (0,) * a.ndim)
    return pl.pallas_call(
        _ple_kernel,
        grid=(n_tok // tm,),
        in_specs=[row(D_MODEL), row(D_MODEL), row(PLE_DIM), full(gple), full(wpg), full(wpp), full(gfin)],
        out_specs=[row(D_MODEL), row(D_MODEL)],
        out_shape=[jax.ShapeDtypeStruct((n_tok, D_MODEL), F32)] * 2,
        compiler_params=_cparams(("parallel",)),
        name="ple",
    )(h1, po, p, gple, wpg, wpp, gfin)


def _pack_w_in(w):
    z = lambda n: jnp.zeros((w.shape[0], n), w.dtype)
    o = np.cumsum([0, 512, 64, 64, 512, 8, 64, 256, 128, 32, 512, 128, 128, 3072])
    aq, ak, av, iq, iw, ik, bq, bkv, bkr, cq, ck, cv, gates = [w[:, o[i]:o[i + 1]] for i in range(13)]
    return jnp.concatenate([aq, ak, av, iq, ik, iw, z(56), bq, bkv, bkr, z(96), cq, ck, cv, gates],
                           axis=1).astype(BF16)


def _pack_wq2(w_uq, w_uk):
    hp = lax.Precision.HIGHEST
    w3 = w_uq.reshape(Q_LORA, N_HEADS, NOPE_DIM + ROPE_DIM)
    lat = jnp.einsum('qhd,chd->qhc', w3[:, :, :NOPE_DIM], w_uk, precision=hp)
    half = ROPE_DIM // 2
    pad = jnp.zeros((Q_LORA, N_HEADS, 128 - ROPE_DIM), F32)
    return jnp.concatenate([lat, w3[:, :, NOPE_DIM:NOPE_DIM + half], w3[:, :, NOPE_DIM + half:], pad],
                           axis=2).reshape(Q_LORA, N_HEADS * 256).astype(BF16)


def _pack_wb(w_uv, w_br_b):
    hp = lax.Precision.HIGHEST
    m = jnp.einsum('chd,hdm->hcm', w_uv, w_br_b.reshape(N_HEADS, HEAD_DIM, D_MODEL), precision=hp)
    return m.reshape(N_HEADS * KV_LORA, D_MODEL).astype(BF16)


def _pack_table(t):
    bits = lax.bitcast_convert_type(t.astype(BF16), jnp.uint16).astype(jnp.uint32)
    half = t.shape[1] // 2
    return jnp.bitwise_or(jnp.left_shift(bits[:, :half], 16), bits[:, half:]).reshape(t.shape[0], 4, 128)


def _rope_tables(pos):
    half = ROPE_DIM // 2
    inv = ROPE_THETA ** (-jnp.arange(half, dtype=F32) / half)
    ang = pos.astype(F32)[:, None] * inv
    cos, sin = jnp.cos(ang), jnp.sin(ang)
    z16 = jnp.zeros_like(cos)
    z96 = jnp.zeros((pos.shape[0], 128 - ROPE_DIM), F32)
    return (jnp.concatenate([cos, cos, z96], axis=1), jnp.concatenate([z16, sin, z96], axis=1),
            jnp.concatenate([-sin, z16, z96], axis=1))


def _t5_bucket(dist):
    n = jnp.maximum(dist, 0)
    exact = N_BUCKETS // 2
    big = exact + (jnp.log(jnp.maximum(n, 1).astype(F32) / exact) / math.log(MAX_DISTANCE / exact)
                   * (N_BUCKETS - exact)).astype(I32)
    return jnp.where(n < exact, n, jnp.minimum(big, N_BUCKETS - 1))


def _prompt_bias_tiles(rel_bias, t):
    i = jnp.arange(t)
    tiles = []
    for delta in range(3):
        d = delta * t + i[:, None] - i[None, :]
        tiles.append(jnp.moveaxis(rel_bias[_t5_bucket(d)], -1, 0))
    return jnp.stack(tiles).astype(F32)


def _sample_bias(rel_bias, past_len, n_tok, n_pages):
    kpos = jnp.arange((n_pages + 1) * 128)
    d = past_len + jnp.arange(n_tok)[:, None] - kpos[None, :]
    b = jnp.moveaxis(rel_bias[_t5_bucket(d)], -1, 0)
    return b.reshape(N_HEADS * n_tok, n_pages + 1, 128).transpose(1, 0, 2).astype(F32)


def _pad_rows(x, rows):
    return jnp.pad(x, ((0, 0), (0, rows - x.shape[1]), (0, 0)))


def kernel(x_prompt, x_sample, cache_a_k, cache_a_v, cache_a_idx, cache_b_ckv, cache_b_krope, cache_c_k,
           cache_c_v, page_table, p_prompt, p_sample, rel_bias, g_mix, w_in, g_q, w_uq, g_kv, w_uk, w_uv,
           w_br_a, w_br_b, w_br_c, w_o, g_ffn, w_pq, sub_k1, sub_k2, peer_u, peer_v, g_ple, w_pg, w_pp,
           g_final):
    batch, s_len, _ = x_prompt.shape
    n_seq, n_tok, _ = x_sample.shape
    depth = g_mix.shape[0]
    n_pages, page = page_table.shape[1], cache_a_k.shape[2]
    past_len = n_pages * page
    n_p, n_s = batch * s_len, n_seq * n_tok
    tm = 256
    t_att = 256
    t_sel = 128
    g_pages = math.gcd(16, n_pages)
    tb_peer = 128

    h = jnp.concatenate([x_prompt.reshape(n_p, D_MODEL), x_sample.reshape(n_s, D_MODEL)], axis=0)
    p_all = jnp.concatenate([p_prompt.reshape(depth, n_p, PLE_DIM), p_sample.reshape(depth, n_s, PLE_DIM)], axis=1)
    pos_rows = jnp.concatenate([jnp.arange(s_len), jnp.tile(past_len + jnp.arange(n_tok), tm // n_tok)])
    rope_tabs = _rope_tables(pos_rows)
    bias_tiles = _prompt_bias_tiles(rel_bias, t_att)
    bias_s = _sample_bias(rel_bias, past_len, n_tok, n_pages)
    cache_c_k2 = cache_c_k.reshape(cache_c_k.shape[:3] + (C_KV_HEADS * HEAD_DIM,))
    cache_c_v2 = cache_c_v.reshape(cache_c_v.shape[:3] + (C_KV_HEADS * HEAD_DIM,))
    row2 = lambda g: g.reshape(1, -1)

    rows_p = [[] for _ in range(7)]
    rows_s = [[] for _ in range(7)]
    y = None
    for i in range(depth):
        (aq, ak, av, iq, ik, iw, qb, ckv, kr, kb, cq, ck, cv, gates) = _project(
            h, row2(g_mix[i]), _pack_w_in(w_in[i]), row2(g_q[i]), _pack_wq2(w_uq[i], w_uk[i]), row2(g_kv[i]),
            rope_tabs, n_p // tm, s_len // tm, tm)
        for lst, r in zip(rows_p, (ak, av, ik, ckv, kr, ck, cv)):
            lst.append(r[:n_p])
        for lst, r in zip(rows_s, (ak, av, ik, ckv, kr, ck, cv)):
            lst.append(r[n_p:])

        mask = _dsa_select(iq, iw, ik, batch, s_len, t_sel)
        oa_p = _dsa_attn(aq, ak, av, mask, bias_tiles, batch, s_len, t_att)
        ob_p = _mla_attn(qb, kb, batch, s_len, t_att)
        oc_p = _sb_attn(cq, ck, cv, batch, s_len, t_att)

        def heads_first(x, width):
            return x[n_p:].reshape(n_seq, n_tok, N_HEADS, width).transpose(0, 2, 1, 3).reshape(
                n_seq, N_HEADS * n_tok, width)

        new = lambda x: _pad_rows(x[n_p:].reshape(n_seq, n_tok, -1), 128)
        addmask = _samp_select(page_table, heads_first(iq, IDX_DIM), heads_first(iw, 1), new(ik),
                               cache_a_idx, i, g_pages)
        oa_s = _samp_dsa(page_table, heads_first(aq, HEAD_DIM), addmask, bias_s, new(ak), new(av),
                         cache_a_k, cache_a_v, i, g_pages)
        ob_s = _samp_mla(page_table, heads_first(qb, 256), new(kb), cache_b_ckv, cache_b_krope, i, g_pages)
        cq_s = cq[n_p:].reshape(n_seq, n_tok, C_KV_HEADS, C_GROUP, HEAD_DIM).transpose(0, 2, 3, 1, 4).reshape(
            n_seq, C_KV_HEADS, C_GROUP * n_tok, HEAD_DIM)
        oc_s = _samp_sb(page_table, cq_s, new(ck), new(cv), cache_c_k2, cache_c_v2, i, g_pages, n_tok)

        def tokens_first(x, width):
            return x.reshape(n_seq, N_HEADS, n_tok, width).transpose(0, 2, 1, 3).reshape(
                n_s, N_HEADS * width).astype(BF16)

        oa = jnp.concatenate([oa_p, tokens_first(oa_s, HEAD_DIM)], axis=0)
        ob = jnp.concatenate([ob_p, tokens_first(ob_s, KV_LORA)], axis=0)
        oc = jnp.concatenate([oc_p, tokens_first(oc_s, HEAD_DIM)], axis=0)

        h1, n2, s_keys = _merge(h, oa, ob, oc, gates, w_br_a[i].astype(BF16), _pack_wb(w_uv[i], w_br_b[i]),
                                w_br_c[i].astype(BF16), w_o[i].astype(BF16), row2(g_ffn[i]),
                                w_pq[i].astype(BF16), sub_k1[i].astype(BF16), sub_k2[i].astype(BF16), tm)
        e_t, g_t = _peer_topk(s_keys, math.gcd(512, n_p + n_s))
        w_t = _peer_u(e_t, n2.reshape(-1, 8, 128), g_t, _pack_table(peer_u[i]), tb_peer)
        po = _peer_v(e_t, w_t, _pack_table(peer_v[i]), tb_peer)
        h, y = _ple(h1, po.reshape(-1, D_MODEL), p_all[i], row2(g_ple[i]), w_pg[i].astype(BF16),
                    w_pp[i].astype(BF16), row2(g_final), tm)

    def stack_rows(lst, lead, tail):
        return jnp.stack(lst, axis=0).reshape((depth,) + lead + tail)

    tails = [(HEAD_DIM,), (HEAD_DIM,), (IDX_DIM,), (KV_LORA,), (ROPE_DIM,), (C_KV_HEADS, HEAD_DIM),
             (C_KV_HEADS, HEAD_DIM)]
    out_p = [stack_rows(l, (batch, s_len), t) for l, t in zip(rows_p, tails)]
    out_s = [stack_rows(l, (n_seq, n_tok), t) for l, t in zip(rows_s, tails)]
    return (y[:n_p].reshape(batch, s_len, D_MODEL), y[n_p:].reshape(n_seq, n_tok, D_MODEL), *out_p, *out_s)
```

```python
import functools
import math

import jax
import jax.numpy as jnp
import numpy as np
from jax import lax
from jax.experimental import pallas as pl
from jax.experimental.pallas import tpu as pltpu

F32 = jnp.float32
BF16 = jnp.bfloat16
I32 = jnp.int32

D_MODEL = 1024
HEAD_DIM = 64
EPS = 1e-6
N_HEADS = 8
IDX_DIM = 64
TOPK_MAX = 256
N_BUCKETS = 32
MAX_DISTANCE = 128
Q_LORA = 256
KV_LORA = 128
NOPE_DIM = 64
ROPE_DIM = 32
ROPE_THETA = 10000.0
MLA_SCALE = (NOPE_DIM + ROPE_DIM) ** -0.5
IDX_SCALE = IDX_DIM ** -0.5 * N_HEADS ** -0.5
C_KV_HEADS = 2
C_GROUP = 4
N_KEYS = 128
PEER_TOPK = 16
PEER_PAIRS = N_HEADS * PEER_TOPK
PLE_DIM = 256

NEG = -1e30
INT_MIN = -(2 ** 31)
VMEM_LIMIT = 56 * 1024 * 1024

C_AQ, C_AKV, C_IQ, C_IKW, C_BQ, C_BKV, C_BKR, C_CQ, C_CK, C_CV, C_GATE, C_END = (
    0, 512, 640, 1152, 1280, 1536, 1664, 1792, 2304, 2432, 2560, 5632)

NT = (((1,), (1,)), ((), ()))


def _cparams(sem):
    return pltpu.CompilerParams(dimension_semantics=sem, vmem_limit_bytes=VMEM_LIMIT)


def _rms(x, g):
    return x * lax.rsqrt(jnp.mean(x * x, axis=-1, keepdims=True) + EPS) * g


def _dot(a, b):
    return jnp.dot(a, b, preferred_element_type=F32)


def _dot_nt(a, b):
    return lax.dot_general(a, b, NT, preferred_element_type=F32)


def _project_kernel(x_ref, gmix_ref, w_ref, gq_ref, wq2_ref, gkv_ref, rc_ref, rsa_ref, rsb_ref,
                    aq_ref, ak_ref, av_ref, iq_ref, ik_ref, iw_ref, qb_ref, ckv_ref, kr_ref, kb_ref,
                    cq_ref, ck_ref, cv_ref, gate_ref):
    n = _rms(x_ref[...], gmix_ref[...]).astype(BF16)

    def seg(a, b):
        return _dot(n, w_ref[:, a:b])

    aq_ref[...] = (seg(C_AQ, C_AKV) * HEAD_DIM ** -0.5).astype(BF16)
    t = seg(C_AKV, C_IQ)
    ak_ref[...] = t[:, :HEAD_DIM]
    av_ref[...] = t[:, HEAD_DIM:]
    iq_ref[...] = seg(C_IQ, C_IKW).astype(BF16)
    t = seg(C_IKW, C_BQ)
    ik_ref[...] = t[:, :IDX_DIM]
    iw_ref[...] = t[:, IDX_DIM:IDX_DIM + N_HEADS] * IDX_SCALE

    rc, rsa, rsb = rc_ref[...], rsa_ref[...], rsb_ref[...]

    def rot(r):
        return r * rc + pltpu.roll(r, 16, 1) * rsa + pltpu.roll(r, 112, 1) * rsb

    nq = _rms(seg(C_BQ, C_BKV), gq_ref[...]).astype(BF16)
    for h in range(N_HEADS):
        z = _dot(nq, wq2_ref[:, h * 256:(h + 1) * 256])
        qb_ref[:, h * 256:h * 256 + 128] = (z[:, :128] * MLA_SCALE).astype(BF16)
        qb_ref[:, h * 256 + 128:(h + 1) * 256] = (rot(z[:, 128:]) * MLA_SCALE).astype(BF16)
    c = _rms(seg(C_BKV, C_BKR), gkv_ref[...])
    ckv_ref[...] = c
    kr = rot(seg(C_BKR, C_CQ))
    kr_ref[...] = kr[:, :ROPE_DIM]
    kb_ref[:, :128] = c.astype(BF16)
    kb_ref[:, 128:] = kr.astype(BF16)
    cq_ref[...] = (seg(C_CQ, C_CK) * HEAD_DIM ** -0.5).astype(BF16)
    ck_ref[...] = seg(C_CK, C_CV)
    cv_ref[...] = seg(C_CV, C_GATE)
    gate_ref[...] = jax.nn.sigmoid(seg(C_GATE, C_END)).astype(BF16)


def _project(h, gmix, w_main, gq, wq2, gkv, rope_tabs, n_prompt_blocks, rope_period, tm):
    n_tok = h.shape[0]
    rc, rsa, rsb = rope_tabs

    def rope_map(i):
        return (jnp.where(i < n_prompt_blocks, i % rope_period, rope_period), 0)

    row = lambda w: pl.BlockSpec((tm, w), lambda i: (i, 0))
    full = lambda a: pl.BlockSpec(a.shape, lambda i: (0,) * a.ndim)
    widths = [(512, BF16), (64, F32), (64, F32), (512, BF16), (64, F32), (8, F32), (2048, BF16),
              (128, F32), (32, F32), (256, BF16), (512, BF16), (128, F32), (128, F32), (3072, BF16)]
    return pl.pallas_call(
        _project_kernel,
        grid=(n_tok // tm,),
        in_specs=[row(D_MODEL), full(gmix), full(w_main), full(gq), full(wq2), full(gkv),
                  pl.BlockSpec((tm, 128), rope_map), pl.BlockSpec((tm, 128), rope_map),
                  pl.BlockSpec((tm, 128), rope_map)],
        out_specs=[row(w) for w, _ in widths],
        out_shape=[jax.ShapeDtypeStruct((n_tok, w), d) for w, d in widths],
        compiler_params=_cparams(("parallel",)),
        name="project",
    )(h, gmix, w_main, gq, wq2, gkv, rc, rsa, rsb)


def _sortable(x):
    b = pltpu.bitcast(x, I32)
    k = jnp.where(b >= 0, b, b ^ 0x7FFFFFFF)
    return jnp.where(b == INT_MIN, 0, k)


def _topk_mask(key_ref, p_ref, k_sel, idx, idx_bits, count):
    kf = float(k_sel)
    zero = count(key_ref[...] >= 0)
    t0 = jnp.where(zero >= kf, 0, INT_MIN).astype(I32)

    def bis(i, t):
        cand = t + jnp.left_shift(jnp.int32(1), 30 - i)
        return jnp.where(count(key_ref[...] >= cand) >= kf, cand, t)

    thr = lax.fori_loop(0, 31, bis, t0)
    key = key_ref[...]
    gt = key > thr
    eq = key == thr
    need = kf - count(gt)
    excess = jnp.logical_and(count(eq) > need, thr != INT_MIN)
    p_ref[...] = jnp.full(p_ref.shape, 2 ** idx_bits, I32)

    @pl.when(jnp.max(excess.astype(F32)) > 0.0)
    def _():
        def bis_idx(i, p):
            cand = p + jnp.left_shift(jnp.int32(1), idx_bits - 1 - i)
            c = count(jnp.logical_and(eq, idx < cand))
            return jnp.where(c < need, cand, p)

        p = lax.fori_loop(0, idx_bits, bis_idx, jnp.zeros(p_ref.shape, I32))
        p_ref[...] = jnp.where(excess, p, 2 ** idx_bits)

    return jnp.logical_or(gt, jnp.logical_and(eq, idx <= p_ref[...]))


def _dsa_select_kernel(iq_ref, iw_ref, ik_ref, mask_ref, key_scr, p_scr, *, k_sel, idx_bits):
    tq, s_len = mask_ref.shape
    qi = pl.program_id(1)
    ik = ik_ref[...].astype(BF16)
    iq = iq_ref[...]
    iw = iw_ref[...]
    sc = jnp.zeros((tq, s_len), F32)
    for h in range(N_HEADS):
        s = _dot_nt(iq[:, h * IDX_DIM:(h + 1) * IDX_DIM], ik)
        sc = sc + iw[:, h:h + 1] * jnp.maximum(s, 0.0)
    row = qi * tq + lax.broadcasted_iota(I32, (tq, s_len), 0)
    col = lax.broadcasted_iota(I32, (tq, s_len), 1)
    causal = col <= row
    key_scr[...] = jnp.where(causal, _sortable(sc), INT_MIN)

    def count(m):
        return jnp.sum(m.astype(F32), axis=1, keepdims=True)

    sel = _topk_mask(key_scr, p_scr, k_sel, col, idx_bits, count)
    mask_ref[...] = jnp.where(jnp.logical_and(sel, causal), 1.0, 0.0).astype(BF16)


def _dsa_select(iq, iw, ik, batch, s_len, tq):
    nq = s_len // tq
    k_sel = min(TOPK_MAX, s_len // 4)
    idx_bits = max(1, (s_len - 1).bit_length())
    return pl.pallas_call(
        functools.partial(_dsa_select_kernel, k_sel=k_sel, idx_bits=idx_bits),
        grid=(batch, nq),
        in_specs=[pl.BlockSpec((tq, 512), lambda b, q: (b * nq + q, 0)),
                  pl.BlockSpec((tq, N_HEADS), lambda b, q: (b * nq + q, 0)),
                  pl.BlockSpec((s_len, IDX_DIM), lambda b, q: (b, 0))],
        out_specs=pl.BlockSpec((tq, s_len), lambda b, q: (b * nq + q, 0)),
        out_shape=jax.ShapeDtypeStruct((batch * s_len, s_len), BF16),
        scratch_shapes=[pltpu.VMEM((tq, s_len), I32), pltpu.VMEM((tq, 1), I32)],
        compiler_params=_cparams(("parallel", "parallel")),
        name="dsa_select",
    )(iq, iw, ik)


def _softmax_update(s, m_old, l_old, acc_old, pv):
    m_new = jnp.maximum(m_old, jnp.max(s, axis=1, keepdims=True))
    alpha = jnp.exp(m_old - m_new)
    p = jnp.exp(s - pltpu.repeat(m_new, s.shape[1] // 128, axis=1))
    l_new = alpha * l_old + jnp.sum(p, axis=1, keepdims=True)
    return m_new, l_new, alpha[:, :acc_old.shape[1]] * acc_old + pv(p.astype(BF16))


def _heads_update(m_scr, l_scr, acc_scr, logits, pv):
    state = [(m_scr[h], l_scr[h], acc_scr[h]) for h in range(N_HEADS)]
    new = [_softmax_update(logits(h), *state[h], pv) for h in range(N_HEADS)]
    for h in range(N_HEADS):
        m_scr[h], l_scr[h], acc_scr[h] = new[h]


def _dsa_attn_kernel(q_ref, k_ref, v_ref, mask_ref, bias_ref, o_ref, qh_scr, m_scr, l_scr, acc_scr):
    qi, ki = pl.program_id(1), pl.program_id(2)
    nk = pl.num_programs(2)

    @pl.when(ki == 0)
    def _():
        m_scr[...] = jnp.full(m_scr.shape, NEG, F32)
        l_scr[...] = jnp.zeros(l_scr.shape, F32)
        acc_scr[...] = jnp.zeros(acc_scr.shape, F32)
        for h in range(N_HEADS):
            qh_scr[h] = q_ref[:, h * HEAD_DIM:(h + 1) * HEAD_DIM]

    @pl.when(ki <= qi)
    def _():
        k = k_ref[...].astype(BF16)
        v = v_ref[...].astype(BF16)
        msk = mask_ref[...] > 0
        tix = jnp.minimum(qi - ki, 2)
        _heads_update(m_scr, l_scr, acc_scr,
                      lambda h: jnp.where(msk, _dot_nt(qh_scr[h], k) + bias_ref[tix, h], NEG),
                      lambda p: _dot(p, v))

    @pl.when(ki == nk - 1)
    def _():
        for h in range(N_HEADS):
            o_ref[:, h * HEAD_DIM:(h + 1) * HEAD_DIM] = (acc_scr[h] / l_scr[h][:, :HEAD_DIM]).astype(BF16)


def _dsa_attn(aq, ak, av, mask, bias_tiles, batch, s_len, t):
    nb = s_len // t
    return pl.pallas_call(
        _dsa_attn_kernel,
        grid=(batch, nb, nb),
        in_specs=[pl.BlockSpec((t, 512), lambda b, q, k: (b * nb + q, 0)),
                  pl.BlockSpec((t, HEAD_DIM), lambda b, q, k: (b * nb + jnp.minimum(k, q), 0)),
                  pl.BlockSpec((t, HEAD_DIM), lambda b, q, k: (b * nb + jnp.minimum(k, q), 0)),
                  pl.BlockSpec((t, t), lambda b, q, k: (b * nb + q, jnp.minimum(k, q))),
                  pl.BlockSpec(bias_tiles.shape, lambda b, q, k: (0, 0, 0, 0))],
        out_specs=pl.BlockSpec((t, 512), lambda b, q, k: (b * nb + q, 0)),
        out_shape=jax.ShapeDtypeStruct((batch * s_len, 512), BF16),
        scratch_shapes=[pltpu.VMEM((N_HEADS, t, HEAD_DIM), BF16), pltpu.VMEM((N_HEADS, t, 128), F32),
                        pltpu.VMEM((N_HEADS, t, 128), F32), pltpu.VMEM((N_HEADS, t, HEAD_DIM), F32)],
        compiler_params=_cparams(("parallel", "parallel", "arbitrary")),
        name="dsa_attn",
    )(aq, ak, av, mask, bias_tiles)


def _mla_attn_kernel(q_ref, kb_ref, o_ref, m_scr, l_scr, acc_scr):
    qi, ki = pl.program_id(1), pl.program_id(2)
    nk = pl.num_programs(2)
    t = kb_ref.shape[0]

    @pl.when(ki == 0)
    def _():
        m_scr[...] = jnp.full(m_scr.shape, NEG, F32)
        l_scr[...] = jnp.zeros(l_scr.shape, F32)
        acc_scr[...] = jnp.zeros(acc_scr.shape, F32)

    @pl.when(ki <= qi)
    def _():
        kb = kb_ref[...]
        v = kb[:, :KV_LORA]
        row = qi * t + lax.broadcasted_iota(I32, (t, t), 0)
        col = ki * t + lax.broadcasted_iota(I32, (t, t), 1)
        msk = col <= row
        _heads_update(m_scr, l_scr, acc_scr,
                      lambda h: jnp.where(msk, _dot_nt(q_ref[:, h * 256:(h + 1) * 256], kb), NEG),
                      lambda p: _dot(p, v))

    @pl.when(ki == nk - 1)
    def _():
        for h in range(N_HEADS):
            o_ref[:, h * KV_LORA:(h + 1) * KV_LORA] = (acc_scr[h] / l_scr[h]).astype(BF16)


def _mla_attn(qb, kb, batch, s_len, t):
    nb = s_len // t
    return pl.pallas_call(
        _mla_attn_kernel,
        grid=(batch, nb, nb),
        in_specs=[pl.BlockSpec((t, 2048), lambda b, q, k: (b * nb + q, 0)),
                  pl.BlockSpec((t, 256), lambda b, q, k: (b * nb + jnp.minimum(k, q), 0))],
        out_specs=pl.BlockSpec((t, 1024), lambda b, q, k: (b * nb + q, 0)),
        out_shape=jax.ShapeDtypeStruct((batch * s_len, 1024), BF16),
        scratch_shapes=[pltpu.VMEM((N_HEADS, t, 128), F32), pltpu.VMEM((N_HEADS, t, 128), F32),
                        pltpu.VMEM((N_HEADS, t, KV_LORA), F32)],
        compiler_params=_cparams(("parallel", "parallel", "arbitrary")),
        name="mla_attn",
    )(qb, kb)


def _sb_block(z, strict, tri, carry, pv):
    lk = -(jnp.maximum(z, 0.0) + jnp.log(1.0 + jnp.exp(-jnp.abs(z))))
    if strict is not None:
        lk = jnp.where(strict, lk, 0.0)
    hi = lk.astype(BF16)
    lo = (lk - hi.astype(F32)).astype(BF16)
    cs = _dot(hi, tri) + _dot(lo, tri)
    a = jnp.exp(z + cs + pltpu.repeat(carry, z.shape[1] // 128, axis=1))
    if strict is not None:
        a = jnp.where(strict, a, 0.0)
    return pv(a.astype(BF16)), carry + jnp.sum(lk, axis=1, keepdims=True)


def _tri(n):
    return jnp.where(lax.broadcasted_iota(I32, (n, n), 0) >= lax.broadcasted_iota(I32, (n, n), 1),
                     1.0, 0.0).astype(BF16)


def _sb_attn_kernel(q_ref, k_ref, v_ref, o_ref, qh_scr, carry_scr, acc_scr):
    qi, j = pl.program_id(1), pl.program_id(2)
    nk = pl.num_programs(2)
    t = k_ref.shape[0]
    ki = qi - j

    @pl.when(j == 0)
    def _():
        carry_scr[...] = jnp.zeros(carry_scr.shape, F32)
        acc_scr[...] = jnp.zeros(acc_scr.shape, F32)
        for h in range(N_HEADS):
            qh_scr[h] = q_ref[:, h * HEAD_DIM:(h + 1) * HEAD_DIM]

    @pl.when(j <= qi)
    def _():
        k = k_ref[...].astype(BF16)
        v = v_ref[...].astype(BF16)
        row = qi * t + lax.broadcasted_iota(I32, (t, t), 0)
        col = ki * t + lax.broadcasted_iota(I32, (t, t), 1)
        strict = col < row
        tri = _tri(t)
        kn = [k[:, n * HEAD_DIM:(n + 1) * HEAD_DIM] for n in range(C_KV_HEADS)]
        vn = [v[:, n * HEAD_DIM:(n + 1) * HEAD_DIM] for n in range(C_KV_HEADS)]
        state = [(carry_scr[h], acc_scr[h]) for h in range(N_HEADS)]
        new = []
        for h in range(N_HEADS):
            n = h // C_GROUP
            o, c = _sb_block(_dot_nt(qh_scr[h], kn[n]), strict, tri, state[h][0], lambda a, n=n: _dot(a, vn[n]))
            new.append((c, state[h][1] + o))
        for h in range(N_HEADS):
            carry_scr[h], acc_scr[h] = new[h]

    @pl.when(j == nk - 1)
    def _():
        for h in range(N_HEADS):
            o_ref[:, h * HEAD_DIM:(h + 1) * HEAD_DIM] = acc_scr[h].astype(BF16)


def _sb_attn(cq, ck, cv, batch, s_len, t):
    nb = s_len // t
    kmap = lambda b, q, j: (b * nb + jnp.maximum(q - j, 0), 0)
    return pl.pallas_call(
        _sb_attn_kernel,
        grid=(batch, nb, nb),
        in_specs=[pl.BlockSpec((t, 512), lambda b, q, j: (b * nb + q, 0)),
                  pl.BlockSpec((t, 128), kmap), pl.BlockSpec((t, 128), kmap)],
        out_specs=pl.BlockSpec((t, 512), lambda b, q, j: (b * nb + q, 0)),
        out_shape=jax.ShapeDtypeStruct((batch * s_len, 512), BF16),
        scratch_shapes=[pltpu.VMEM((N_HEADS, t, HEAD_DIM), BF16), pltpu.VMEM((N_HEADS, t, 128), F32),
                        pltpu.VMEM((N_HEADS, t, HEAD_DIM), F32)],
        compiler_params=_cparams(("parallel", "parallel", "arbitrary")),
        name="sb_attn",
    )(cq, ck, cv)


def _page_specs(cache, layer, g_pages, n_pages, page_of):
    page, width = cache.shape[2], cache.shape[3]

    def spec(g):
        return pl.BlockSpec((None, None, page, width),
                            lambda b, j, pt: (layer, pt[b * n_pages + page_of(j, g)], 0, 0))

    return [spec(g) for g in range(g_pages)]


def _samp_select_kernel(pt_ref, iq_ref, iw_ref, iknew_ref, *rest, g_pages, n_pages, k_sel, idx_bits):
    pages = rest[:g_pages]
    out_ref, sc_scr, key_scr, p_scr = rest[g_pages:]
    j = pl.program_id(1)
    ns = pl.num_programs(1)
    q = iq_ref[...]
    w = iw_ref[...]
    n_tok = out_ref.shape[0]

    def score(kblk_t):
        s = jnp.maximum(_dot(q, kblk_t.astype(BF16)), 0.0) * w
        tot = s[0:n_tok]
        for h in range(1, N_HEADS):
            tot = tot + s[h * n_tok:(h + 1) * n_tok]
        return tot

    for g in range(g_pages):
        sc_scr[j * g_pages + g] = score(pages[g][...])

    @pl.when(j == ns - 1)
    def _():
        sc_scr[n_pages] = score(iknew_ref[...])
        shape = sc_scr.shape
        page_i = lax.broadcasted_iota(I32, shape, 0)
        row = lax.broadcasted_iota(I32, shape, 1)
        lane = lax.broadcasted_iota(I32, shape, 2)
        causal = jnp.logical_or(page_i < n_pages, lane <= row)
        key_scr[...] = jnp.where(causal, _sortable(sc_scr[...]), INT_MIN)

        def count(m):
            return jnp.sum(jnp.sum(m.astype(F32), axis=0), axis=1, keepdims=True)[None]

        sel = _topk_mask(key_scr, p_scr, k_sel, page_i * 128 + lane, idx_bits, count)
        key_scr[...] = jnp.where(jnp.logical_and(sel, causal), 1, 0)
        for p in range(n_pages + 1):
            out_ref[:, p * 128:(p + 1) * 128] = jnp.where(key_scr[p] > 0, 0.0, NEG)


def _samp_select(page_table, iq_s, iw_s, ik_new, cache_idx, layer, g_pages):
    n_seq, n_tok = ik_new.shape[0], iq_s.shape[1] // N_HEADS
    n_pages = page_table.shape[1]
    ns = n_pages // g_pages
    n_keys = n_pages * 128 + n_tok
    k_sel = min(TOPK_MAX, n_keys // 4)
    idx_bits = ((n_pages + 1) * 128 - 1).bit_length()
    grid_spec = pltpu.PrefetchScalarGridSpec(
        num_scalar_prefetch=1,
        grid=(n_seq, ns),
        in_specs=[pl.BlockSpec((None, N_HEADS * n_tok, IDX_DIM), lambda b, j, pt: (b, 0, 0)),
                  pl.BlockSpec((None, N_HEADS * n_tok, 1), lambda b, j, pt: (b, 0, 0)),
                  pl.BlockSpec((None, IDX_DIM, 128), lambda b, j, pt: (b, 0, 0))]
        + _page_specs(cache_idx, layer, g_pages, n_pages, lambda j, g: j * g_pages + g),
        out_specs=pl.BlockSpec((None, n_tok, (n_pages + 1) * 128), lambda b, j, pt: (b, 0, 0)),
        scratch_shapes=[pltpu.VMEM((n_pages + 1, n_tok, 128), F32),
                        pltpu.VMEM((n_pages + 1, n_tok, 128), I32),
                        pltpu.VMEM((1, n_tok, 1), I32)],
    )
    return pl.pallas_call(
        functools.partial(_samp_select_kernel, g_pages=g_pages, n_pages=n_pages, k_sel=k_sel,
                          idx_bits=idx_bits),
        grid_spec=grid_spec,
        out_shape=jax.ShapeDtypeStruct((n_seq, n_tok, (n_pages + 1) * 128), F32),
        compiler_params=_cparams(("parallel", "arbitrary")),
        name="samp_select",
    )(page_table.reshape(-1), iq_s, iw_s, ik_new, *([cache_idx] * g_pages))


def _state_update(s, m_scr, l_scr, acc_scr, pv):
    m_scr[...], l_scr[...], acc_scr[...] = _softmax_update(s, m_scr[...], l_scr[...], acc_scr[...], pv)


def _init_softmax_state(m_scr, l_scr, acc_scr):
    m_scr[...] = jnp.full(m_scr.shape, NEG, F32)
    l_scr[...] = jnp.zeros(l_scr.shape, F32)
    acc_scr[...] = jnp.zeros(acc_scr.shape, F32)


def _lanes(x, g):
    return x[:, g * 128:(g + 1) * 128]


def _samp_dsa_kernel(pt_ref, q_ref, am_ref, amnew_ref, bias_ref, biasnew_ref, knew_ref, vnew_ref, *rest,
                     g_pages):
    kp = rest[:g_pages]
    vp = rest[g_pages:2 * g_pages]
    o_ref, m_scr, l_scr, acc_scr = rest[2 * g_pages:]
    j = pl.program_id(1)
    ns = pl.num_programs(1)
    q = q_ref[...]

    @pl.when(j == 0)
    def _():
        _init_softmax_state(m_scr, l_scr, acc_scr)

    s = jnp.concatenate([_dot(q, kp[g][...].astype(BF16)) for g in range(g_pages)], axis=1)
    s = s + bias_ref[...] + jnp.concatenate([am_ref[...]] * N_HEADS, axis=0)

    def pv(p):
        out = _dot_nt(_lanes(p, 0), vp[0][...].astype(BF16))
        for g in range(1, g_pages):
            out = out + _dot_nt(_lanes(p, g), vp[g][...].astype(BF16))
        return out

    _state_update(s, m_scr, l_scr, acc_scr, pv)

    @pl.when(j == ns - 1)
    def _():
        s_new = (_dot(q, knew_ref[...].astype(BF16)) + biasnew_ref[...]
                 + jnp.concatenate([amnew_ref[...]] * N_HEADS, axis=0))
        _state_update(s_new, m_scr, l_scr, acc_scr, lambda p: _dot_nt(p, vnew_ref[...].astype(BF16)))
        o_ref[...] = acc_scr[...] / l_scr[...][:, :HEAD_DIM]


def _samp_dsa(page_table, q_s, addmask, bias_s, k_new, v_new, cache_k, cache_v, layer, g_pages):
    n_seq, rows = q_s.shape[0], q_s.shape[1]
    n_pages = page_table.shape[1]
    n_tok = rows // N_HEADS
    wide = g_pages * 128
    page_of = lambda j, g: j * g_pages + g
    grid_spec = pltpu.PrefetchScalarGridSpec(
        num_scalar_prefetch=1,
        grid=(n_seq, n_pages // g_pages),
        in_specs=[pl.BlockSpec((None, rows, HEAD_DIM), lambda b, j, pt: (b, 0, 0)),
                  pl.BlockSpec((None, n_tok, wide), lambda b, j, pt: (b, 0, j)),
                  pl.BlockSpec((None, n_tok, 128), lambda b, j, pt: (b, 0, n_pages)),
                  pl.BlockSpec((rows, wide), lambda b, j, pt: (0, j)),
                  pl.BlockSpec((rows, 128), lambda b, j, pt: (0, n_pages)),
                  pl.BlockSpec((None, HEAD_DIM, 128), lambda b, j, pt: (b, 0, 0)),
                  pl.BlockSpec((None, HEAD_DIM, 128), lambda b, j, pt: (b, 0, 0))]
        + _page_specs(cache_k, layer, g_pages, n_pages, page_of)
        + _page_specs(cache_v, layer, g_pages, n_pages, page_of),
        out_specs=pl.BlockSpec((None, rows, HEAD_DIM), lambda b, j, pt: (b, 0, 0)),
        scratch_shapes=[pltpu.VMEM((rows, 128), F32), pltpu.VMEM((rows, 128), F32),
                        pltpu.VMEM((rows, HEAD_DIM), F32)],
    )
    return pl.pallas_call(
        functools.partial(_samp_dsa_kernel, g_pages=g_pages),
        grid_spec=grid_spec,
        out_shape=jax.ShapeDtypeStruct((n_seq, rows, HEAD_DIM), F32),
        compiler_params=_cparams(("parallel", "arbitrary")),
        name="samp_dsa",
    )(page_table.reshape(-1), q_s, addmask, addmask, bias_s, bias_s, k_new, v_new,
      *([cache_k] * g_pages), *([cache_v] * g_pages))


def _samp_mla_kernel(pt_ref, q_ref, kbnew_ref, *rest, g_pages, n_tok):
    cp = rest[:g_pages]
    rp = rest[g_pages:2 * g_pages]
    o_ref, m_scr, l_scr, acc_scr = rest[2 * g_pages:]
    j = pl.program_id(1)
    ns = pl.num_programs(1)
    q = q_ref[...]

    @pl.when(j == 0)
    def _():
        _init_softmax_state(m_scr, l_scr, acc_scr)

    cs = [cp[g][...].astype(BF16) for g in range(g_pages)]
    q_lat, q_rope = q[:, :KV_LORA], q[:, KV_LORA:KV_LORA + ROPE_DIM]
    s = jnp.concatenate([_dot_nt(q_lat, cs[g]) + _dot(q_rope, rp[g][...].astype(BF16))
                         for g in range(g_pages)], axis=1)

    def pv(p):
        out = _dot(_lanes(p, 0), cs[0])
        for g in range(1, g_pages):
            out = out + _dot(_lanes(p, g), cs[g])
        return out

    _state_update(s, m_scr, l_scr, acc_scr, pv)

    @pl.when(j == ns - 1)
    def _():
        kb = kbnew_ref[...]
        s_new = _dot_nt(q, kb)
        row = lax.broadcasted_iota(I32, s_new.shape, 0) % n_tok
        col = lax.broadcasted_iota(I32, s_new.shape, 1)
        _state_update(jnp.where(col <= row, s_new, NEG), m_scr, l_scr, acc_scr,
                      lambda p: _dot(p, kb[:, :KV_LORA]))
        o_ref[...] = acc_scr[...] / l_scr[...]


def _samp_mla(page_table, q_s, kb_new, cache_ckv, cache_kr, layer, g_pages):
    n_seq, rows = q_s.shape[0], q_s.shape[1]
    n_pages = page_table.shape[1]
    page_of = lambda j, g: j * g_pages + g
    grid_spec = pltpu.PrefetchScalarGridSpec(
        num_scalar_prefetch=1,
        grid=(n_seq, n_pages // g_pages),
        in_specs=[pl.BlockSpec((None, rows, 256), lambda b, j, pt: (b, 0, 0)),
                  pl.BlockSpec((None, 128, 256), lambda b, j, pt: (b, 0, 0))]
        + _page_specs(cache_ckv, layer, g_pages, n_pages, page_of)
        + _page_specs(cache_kr, layer, g_pages, n_pages, page_of),
        out_specs=pl.BlockSpec((None, rows, KV_LORA), lambda b, j, pt: (b, 0, 0)),
        scratch_shapes=[pltpu.VMEM((rows, 128), F32), pltpu.VMEM((rows, 128), F32),
                        pltpu.VMEM((rows, KV_LORA), F32)],
    )
    return pl.pallas_call(
        functools.partial(_samp_mla_kernel, g_pages=g_pages, n_tok=rows // N_HEADS),
        grid_spec=grid_spec,
        out_shape=jax.ShapeDtypeStruct((n_seq, rows, KV_LORA), F32),
        compiler_params=_cparams(("parallel", "arbitrary")),
        name="samp_mla",
    )(page_table.reshape(-1), q_s, kb_new, *([cache_ckv] * g_pages), *([cache_kr] * g_pages))


def _samp_sb_kernel(pt_ref, q_ref, knew_ref, vnew_ref, *rest, g_pages, n_tok):
    kp = rest[:g_pages]
    vp = rest[g_pages:2 * g_pages]
    o_ref, carry_scr, acc_scr = rest[2 * g_pages:]
    j = pl.program_id(1)
    ns = pl.num_programs(1)
    rows = q_ref.shape[1]
    tri = _tri(128)
    head = lambda x, n: x[n * HEAD_DIM:(n + 1) * HEAD_DIM, :].astype(BF16)

    @pl.when(j == 0)
    def _():
        carry_scr[...] = jnp.zeros(carry_scr.shape, F32)
        acc_scr[...] = jnp.zeros(acc_scr.shape, F32)
        row = lax.broadcasted_iota(I32, (rows, 128), 0) % n_tok
        col = lax.broadcasted_iota(I32, (rows, 128), 1)
        for n in range(C_KV_HEADS):
            o, c = _sb_block(_dot(q_ref[n], head(knew_ref[...], n)), col < row, tri, carry_scr[n],
                             lambda a, n=n: _dot_nt(a, head(vnew_ref[...], n)))
            acc_scr[n] = o
            carry_scr[n] = c

    order = list(reversed(range(g_pages)))
    for n in range(C_KV_HEADS):
        z = [_dot(q_ref[n], head(kp[g][...], n)) for g in order]
        lk = [-(jnp.maximum(x, 0.0) + jnp.log(1.0 + jnp.exp(-jnp.abs(x)))) for x in z]
        lk_all = jnp.concatenate(lk, axis=0)
        hi = lk_all.astype(BF16)
        lo = (lk_all - hi.astype(F32)).astype(BF16)
        cs = _dot(hi, tri) + _dot(lo, tri)
        carry = carry_scr[n]
        out = acc_scr[n]
        for i, g in enumerate(order):
            a = jnp.exp(z[i] + cs[i * rows:(i + 1) * rows] + carry)
            out = out + _dot_nt(a.astype(BF16), head(vp[g][...], n))
            carry = carry + jnp.sum(lk[i], axis=1, keepdims=True)
        acc_scr[n] = out
        carry_scr[n] = carry

    @pl.when(j == ns - 1)
    def _():
        o_ref[...] = acc_scr[...]


def _samp_sb(page_table, q_s, k_new, v_new, cache_k, cache_v, layer, g_pages, n_tok):
    n_seq, rows = q_s.shape[0], q_s.shape[2]
    n_pages = page_table.shape[1]
    ns = n_pages // g_pages
    page_of = lambda j, g: (ns - 1 - j) * g_pages + g
    grid_spec = pltpu.PrefetchScalarGridSpec(
        num_scalar_prefetch=1,
        grid=(n_seq, ns),
        in_specs=[pl.BlockSpec((None, C_KV_HEADS, rows, HEAD_DIM), lambda b, j, pt: (b, 0, 0, 0)),
                  pl.BlockSpec((None, 128, 128), lambda b, j, pt: (b, 0, 0)),
                  pl.BlockSpec((None, 128, 128), lambda b, j, pt: (b, 0, 0))]
        + _page_specs(cache_k, layer, g_pages, n_pages, page_of)
        + _page_specs(cache_v, layer, g_pages, n_pages, page_of),
        out_specs=pl.BlockSpec((None, C_KV_HEADS, rows, HEAD_DIM), lambda b, j, pt: (b, 0, 0, 0)),
        scratch_shapes=[pltpu.VMEM((C_KV_HEADS, rows, 128), F32),
                        pltpu.VMEM((C_KV_HEADS, rows, HEAD_DIM), F32)],
    )
    return pl.pallas_call(
        functools.partial(_samp_sb_kernel, g_pages=g_pages, n_tok=n_tok),
        grid_spec=grid_spec,
        out_shape=jax.ShapeDtypeStruct((n_seq, C_KV_HEADS, rows, HEAD_DIM), F32),
        compiler_params=_cparams(("parallel", "arbitrary")),
        name="samp_sb",
    )(page_table.reshape(-1), q_s, k_new, v_new, *([cache_k] * g_pages), *([cache_v] * g_pages))


def _merge_kernel(h_ref, oa_ref, ob_ref, oc_ref, gate_ref, wa_ref, wb_ref, wc_ref, wo_ref, gffn_ref,
                  wpq_ref, sk1_ref, sk2_ref, h1_ref, n2_ref, s_ref):
    g = gate_ref[...].astype(F32)
    m = (g[:, :D_MODEL] * _dot(oa_ref[...], wa_ref[...])
         + g[:, D_MODEL:2 * D_MODEL] * _dot(ob_ref[...], wb_ref[...])
         + g[:, 2 * D_MODEL:] * _dot(oc_ref[...], wc_ref[...]))
    h1 = h_ref[...] + _dot(m.astype(BF16), wo_ref[...])
    h1_ref[...] = h1
    n2 = _rms(h1, gffn_ref[...])
    n2_ref[...] = n2
    q = _dot(n2.astype(BF16), wpq_ref[...])
    for h in range(N_HEADS):
        for j, sk in enumerate((sk1_ref, sk2_ref)):
            qh = q[:, h * 256 + j * 128:h * 256 + (j + 1) * 128].astype(BF16)
            s_ref[j, h] = _dot_nt(sk[...], qh)


def _merge(h, oa, ob, oc, gates, wa, wb, wc, wo, gffn, wpq, sk1, sk2, tm):
    n_tok = h.shape[0]
    row = lambda w: pl.BlockSpec((tm, w), lambda i: (i, 0))
    full = lambda a: pl.BlockSpec(a.shape, lambda i: (0,) * a.ndim)
    return pl.pallas_call(
        _merge_kernel,
        grid=(n_tok // tm,),
        in_specs=[row(D_MODEL), row(512), row(1024), row(512), row(3072), full(wa), full(wb), full(wc),
                  full(wo), full(gffn), full(wpq), full(sk1), full(sk2)],
        out_specs=[row(D_MODEL), row(D_MODEL),
                   pl.BlockSpec((2, N_HEADS, N_KEYS, tm), lambda i: (0, 0, 0, i))],
        out_shape=[jax.ShapeDtypeStruct((n_tok, D_MODEL), F32), jax.ShapeDtypeStruct((n_tok, D_MODEL), F32),
                   jax.ShapeDtypeStruct((2, N_HEADS, N_KEYS, n_tok), F32)],
        compiler_params=_cparams(("parallel",)),
        name="merge",
    )(h, oa, ob, oc, gates, wa, wb, wc, wo, gffn, wpq, sk1, sk2)


def _extract16(s, ids):
    vals, idxs = [], []
    for _ in range(PEER_TOPK):
        m = jnp.max(s, axis=0, keepdims=True)
        idx = jnp.min(jnp.where(s == m, ids, 2 ** 30), axis=0, keepdims=True)
        vals.append(m)
        idxs.append(idx)
        s = jnp.where(ids == idx, -jnp.inf, s)
    return vals, idxs


def _peer_topk_kernel(s_ref, e_ref, g_ref):
    cols = s_ref.shape[-1]
    kio = lax.broadcasted_iota(I32, (N_KEYS, 128), 0)
    for c in range(cols // 128):
        sl = slice(c * 128, (c + 1) * 128)
        v1, i1 = _extract16(s_ref[0, 0, :, sl], kio)
        v2, i2 = _extract16(s_ref[1, 0, :, sl], kio)
        v1a, i1a = jnp.concatenate(v1, axis=0), jnp.concatenate(i1, axis=0)
        v2a, i2a = jnp.concatenate(v2, axis=0), jnp.concatenate(i2, axis=0)
        cv = [v1[0] + v2a] + [v1[a] + v2a[:8] for a in range(1, 8)] + [v1a[8:] + v2[0]]
        ci = [i1[0] * N_KEYS + i2a] + [i1[a] * N_KEYS + i2a[:8] for a in range(1, 8)] + [i1a[8:] * N_KEYS + i2[0]]
        sc, e = _extract16(jnp.concatenate(cv, axis=0), jnp.concatenate(ci, axis=0))
        sc = jnp.concatenate(sc, axis=0)
        p = jnp.exp(sc - sc[0:1])
        e_ref[:, sl] = jnp.concatenate(e, axis=0)
        g_ref[:, sl] = p / jnp.sum(p, axis=0, keepdims=True)


def _peer_topk(s, tc):
    n_tok = s.shape[-1]
    return pl.pallas_call(
        _peer_topk_kernel,
        grid=(N_HEADS, n_tok // tc),
        in_specs=[pl.BlockSpec((2, 1, N_KEYS, tc), lambda h, i: (0, h, 0, i))],
        out_specs=[pl.BlockSpec((PEER_TOPK, tc), lambda h, i: (h, i)),
                   pl.BlockSpec((PEER_TOPK, tc), lambda h, i: (h, i))],
        out_shape=[jax.ShapeDtypeStruct((PEER_PAIRS, n_tok), I32),
                   jax.ShapeDtypeStruct((PEER_PAIRS, n_tok), F32)],
        compiler_params=_cparams(("parallel", "parallel")),
        name="peer_topk",
    )(s)


def _unpack(w):
    hi = pltpu.bitcast(jnp.bitwise_and(w, jnp.uint32(0xFFFF0000)), F32)
    lo = pltpu.bitcast(jnp.left_shift(w, jnp.uint32(16)), F32)
    return hi, lo


def _peer_u_kernel(e_ref, x_ref, g_ref, tab_ref, w_ref, p0_scr, p1_scr):
    tb = x_ref.shape[0]
    lane = lax.broadcasted_iota(I32, (PEER_PAIRS, tb), 1)

    def gather(t, p_scr):
        x = x_ref[t]
        xh, xl = x[0:4], x[4:8]
        for j in range(PEER_PAIRS):
            hi, lo = _unpack(tab_ref[e_ref[t, j]])
            p_scr[4 * j:4 * j + 4, :] = hi * xh + lo * xl

    def reduce(p_scr, t, acts):
        r = (p_scr[pl.ds(0, PEER_PAIRS, stride=4), :] + p_scr[pl.ds(1, PEER_PAIRS, stride=4), :]
             + p_scr[pl.ds(2, PEER_PAIRS, stride=4), :] + p_scr[pl.ds(3, PEER_PAIRS, stride=4), :])
        return jnp.where(lane == t, jnp.sum(r, axis=1, keepdims=True), acts)

    def two_tokens(i, acts):
        gather(2 * i, p0_scr)
        acts = reduce(p1_scr, 2 * i - 1, acts)
        gather(2 * i + 1, p1_scr)
        return reduce(p0_scr, 2 * i, acts)

    p1_scr[...] = jnp.zeros(p1_scr.shape, F32)
    a = lax.fori_loop(0, tb // 2, two_tokens, jnp.zeros((PEER_PAIRS, tb), F32))
    a = reduce(p1_scr, tb - 1, a)
    w_ref[...] = g_ref[...] * jax.nn.gelu(a)


def _peer_u(e_tok, x3, g_t, tab, tb):
    n_tok = x3.shape[0]
    return pl.pallas_call(
        _peer_u_kernel,
        grid=(n_tok // tb,),
        in_specs=[pl.BlockSpec((tb, PEER_PAIRS), lambda i: (i, 0), memory_space=pltpu.SMEM),
                  pl.BlockSpec((tb, 8, 128), lambda i: (i, 0, 0)),
                  pl.BlockSpec((PEER_PAIRS, tb), lambda i: (0, i)),
                  pl.BlockSpec(memory_space=pltpu.VMEM)],
        out_specs=pl.BlockSpec((PEER_PAIRS, tb), lambda i: (0, i)),
        out_shape=jax.ShapeDtypeStruct((PEER_PAIRS, n_tok), F32),
        scratch_shapes=[pltpu.VMEM((4 * PEER_PAIRS, 128), F32), pltpu.VMEM((4 * PEER_PAIRS, 128), F32)],
        compiler_params=_cparams(("arbitrary",)),
        name="peer_u",
    )(e_tok, x3, g_t, tab)


def _peer_v_kernel(e_ref, w_ref, tab_ref, o_ref, wrep_scr):
    tb = o_ref.shape[0]
    lane = lax.broadcasted_iota(I32, (PEER_PAIRS, tb), 1)

    def stage_weights(t):
        col = jnp.sum(jnp.where(lane == t, w_ref[...], 0.0), axis=1, keepdims=True)
        wrep_scr[t % 2] = jnp.broadcast_to(col, (PEER_PAIRS, 128))

    def tok(t, carry):
        stage_weights(t + 1)
        wrep = wrep_scr.at[t % 2]
        acc_h = [jnp.zeros((4, 128), F32) for _ in range(4)]
        acc_l = [jnp.zeros((4, 128), F32) for _ in range(4)]
        for j in range(PEER_PAIRS):
            hi, lo = _unpack(tab_ref[e_ref[t, j]])
            wgt = jnp.broadcast_to(wrep[j:j + 1, :], (4, 128))
            acc_h[j % 4] = acc_h[j % 4] + wgt * hi
            acc_l[j % 4] = acc_l[j % 4] + wgt * lo
        o_ref[t] = jnp.concatenate([(acc_h[0] + acc_h[1]) + (acc_h[2] + acc_h[3]),
                                    (acc_l[0] + acc_l[1]) + (acc_l[2] + acc_l[3])], axis=0)
        return carry

    stage_weights(0)
    lax.fori_loop(0, tb, tok, 0)


def _peer_v(e_tok, w_t, tab, tb):
    n_tok = e_tok.shape[0]
    return pl.pallas_call(
        _peer_v_kernel,
        grid=(n_tok // tb,),
        in_specs=[pl.BlockSpec((tb, PEER_PAIRS), lambda i: (i, 0), memory_space=pltpu.SMEM),
                  pl.BlockSpec((PEER_PAIRS, tb), lambda i: (0, i)),
                  pl.BlockSpec(memory_space=pltpu.VMEM)],
        out_specs=pl.BlockSpec((tb, 8, 128), lambda i: (i, 0, 0)),
        out_shape=jax.ShapeDtypeStruct((n_tok, 8, 128), F32),
        scratch_shapes=[pltpu.VMEM((2, PEER_PAIRS, 128), F32)],
        compiler_params=_cparams(("arbitrary",)),
        name="peer_v",
    )(e_tok, w_t, tab)


def _ple_kernel(h1_ref, po_ref, p_ref, gple_ref, wpg_ref, wpp_ref, gfin_ref, h_ref, y_ref):
    h2 = h1_ref[...] + po_ref[...]
    n = _rms(h2, gple_ref[...]).astype(BF16)
    h3 = h2 + jax.nn.sigmoid(_dot(n, wpg_ref[...])) * _dot(p_ref[...].astype(BF16), wpp_ref[...])
    h_ref[...] = h3
    y_ref[...] = _rms(h3, gfin_ref[...])


def _ple(h1, po, p, gple, wpg, wpp, gfin, tm):
    n_tok = h1.shape[0]
    row = lambda w: pl.BlockSpec((tm, w), lambda i: (i, 0))
    full = lambda a: pl.BlockSpec(a.shape, lambda i: (0,) * a.ndim)
    return pl.pallas_call(
        _ple_kernel,
        grid=(n_tok // tm,),
        in_specs=[row(D_MODEL), row(D_MODEL), row(PLE_DIM), full(gple), full(wpg), full(wpp), full(gfin)],
        out_specs=[row(D_MODEL), row(D_MODEL)],
        out_shape=[jax.ShapeDtypeStruct((n_tok, D_MODEL), F32)] * 2,
        compiler_params=_cparams(("parallel",)),
        name="ple",
    )(h1, po, p, gple, wpg, wpp, gfin)


def _pack_w_in(w):
    z = lambda n: jnp.zeros((w.shape[0], n), w.dtype)
    o = np.cumsum([0, 512, 64, 64, 512, 8, 64, 256, 128, 32, 512, 128, 128, 3072])
    aq, ak, av, iq, iw, ik, bq, bkv, bkr, cq, ck, cv, gates = [w[:, o[i]:o[i + 1]] for i in range(13)]
    return jnp.concatenate([aq, ak, av, iq, ik, iw, z(56), bq, bkv, bkr, z(96), cq, ck, cv, gates],
                           axis=1).astype(BF16)


def _pack_wq2(w_uq, w_uk):
    hp = lax.Precision.HIGHEST
    w3 = w_uq.reshape(Q_LORA, N_HEADS, NOPE_DIM + ROPE_DIM)
    lat = jnp.einsum('qhd,chd->qhc', w3[:, :, :NOPE_DIM], w_uk, precision=hp)
    half = ROPE_DIM // 2
    pad = jnp.zeros((Q_LORA, N_HEADS, 128 - ROPE_DIM), F32)
    return jnp.concatenate([lat, w3[:, :, NOPE_DIM:NOPE_DIM + half], w3[:, :, NOPE_DIM + half:], pad],
                           axis=2).reshape(Q_LORA, N_HEADS * 256).astype(BF16)


def _pack_wb(w_uv, w_br_b):
    hp = lax.Precision.HIGHEST
    m = jnp.einsum('chd,hdm->hcm', w_uv, w_br_b.reshape(N_HEADS, HEAD_DIM, D_MODEL), precision=hp)
    return m.reshape(N_HEADS * KV_LORA, D_MODEL).astype(BF16)


def _pack_table(t):
    bits = lax.bitcast_convert_type(t.astype(BF16), jnp.uint16).astype(jnp.uint32)
    half = t.shape[1] // 2
    return jnp.bitwise_or(jnp.left_shift(bits[:, :half], 16), bits[:, half:]).reshape(t.shape[0], 4, 128)


def _rope_tables(pos):
    half = ROPE_DIM // 2
    inv = ROPE_THETA ** (-jnp.arange(half, dtype=F32) / half)
    ang = pos.astype(F32)[:, None] * inv
    cos, sin = jnp.cos(ang), jnp.sin(ang)
    z16 = jnp.zeros_like(cos)
    z96 = jnp.zeros((pos.shape[0], 128 - ROPE_DIM), F32)
    return (jnp.concatenate([cos, cos, z96], axis=1), jnp.concatenate([z16, sin, z96], axis=1),
            jnp.concatenate([-sin, z16, z96], axis=1))


def _t5_bucket(dist):
    n = jnp.maximum(dist, 0)
    exact = N_BUCKETS // 2
    big = exact + (jnp.log(jnp.maximum(n, 1).astype(F32) / exact) / math.log(MAX_DISTANCE / exact)
                   * (N_BUCKETS - exact)).astype(I32)
    return jnp.where(n < exact, n, jnp.minimum(big, N_BUCKETS - 1))


def _prompt_bias_tiles(rel_bias, t):
    i = jnp.arange(t)
    tiles = []
    for delta in range(3):
        d = delta * t + i[:, None] - i[None, :]
        tiles.append(jnp.moveaxis(rel_bias[_t5_bucket(d)], -1, 0))
    return jnp.stack(tiles).astype(F32)


def _sample_bias(rel_bias, past_len, n_tok, n_pages):
    kpos = jnp.arange((n_pages + 1) * 128)
    d = past_len + jnp.arange(n_tok)[:, None] - kpos[None, :]
    b = jnp.moveaxis(rel_bias[_t5_bucket(d)], -1, 0)
    return b.reshape(N_HEADS * n_tok, (n_pages + 1) * 128).astype(F32)


def _pad_rows(x, rows):
    return jnp.pad(x, ((0, 0), (0, rows - x.shape[1]), (0, 0)))


def _keys_on_lanes(x, n_seq, n_tok):
    xt = x.reshape(n_seq, n_tok, -1).transpose(0, 2, 1)
    return jnp.pad(xt, ((0, 0), (0, 0), (0, 128 - n_tok)))


def kernel(x_prompt, x_sample, cache_a_k, cache_a_v, cache_a_idx, cache_b_ckv, cache_b_krope, cache_c_k,
           cache_c_v, page_table, p_prompt, p_sample, rel_bias, g_mix, w_in, g_q, w_uq, g_kv, w_uk, w_uv,
           w_br_a, w_br_b, w_br_c, w_o, g_ffn, w_pq, sub_k1, sub_k2, peer_u, peer_v, g_ple, w_pg, w_pp,
           g_final):
    batch, s_len, _ = x_prompt.shape
    n_seq, n_tok, _ = x_sample.shape
    depth = g_mix.shape[0]
    n_pages, page = page_table.shape[1], cache_a_k.shape[2]
    past_len = n_pages * page
    n_p, n_s = batch * s_len, n_seq * n_tok
    tm = 256
    t_att = 256
    t_sel = 128
    g_pages = math.gcd(16, n_pages)
    tb_peer = 128

    h = jnp.concatenate([x_prompt.reshape(n_p, D_MODEL), x_sample.reshape(n_s, D_MODEL)], axis=0)
    p_all = jnp.concatenate([p_prompt.reshape(depth, n_p, PLE_DIM), p_sample.reshape(depth, n_s, PLE_DIM)], axis=1)
    pos_rows = jnp.concatenate([jnp.arange(s_len), jnp.tile(past_len + jnp.arange(n_tok), tm // n_tok)])
    rope_tabs = _rope_tables(pos_rows)
    bias_tiles = _prompt_bias_tiles(rel_bias, t_att)
    bias_s = _sample_bias(rel_bias, past_len, n_tok, n_pages)
    t_a_k, t_a_v, t_a_idx, t_b_kr = (jnp.swapaxes(c, 2, 3) for c in (cache_a_k, cache_a_v, cache_a_idx, cache_b_krope))
    t_c_k, t_c_v = (jnp.transpose(c, (0, 1, 3, 4, 2)).reshape(c.shape[:2] + (C_KV_HEADS * HEAD_DIM, page))
                    for c in (cache_c_k, cache_c_v))
    row2 = lambda g: g.reshape(1, -1)

    rows_p = [[] for _ in range(7)]
    rows_s = [[] for _ in range(7)]
    y = None
    for i in range(depth):
        (aq, ak, av, iq, ik, iw, qb, ckv, kr, kb, cq, ck, cv, gates) = _project(
            h, row2(g_mix[i]), _pack_w_in(w_in[i]), row2(g_q[i]), _pack_wq2(w_uq[i], w_uk[i]), row2(g_kv[i]),
            rope_tabs, n_p // tm, s_len // tm, tm)
        for lst, r in zip(rows_p, (ak, av, ik, ckv, kr, ck, cv)):
            lst.append(r[:n_p])
        for lst, r in zip(rows_s, (ak, av, ik, ckv, kr, ck, cv)):
            lst.append(r[n_p:])

        mask = _dsa_select(iq, iw, ik, batch, s_len, t_sel)
        oa_p = _dsa_attn(aq, ak, av, mask, bias_tiles, batch, s_len, t_att)
        ob_p = _mla_attn(qb, kb, batch, s_len, t_att)
        oc_p = _sb_attn(cq, ck, cv, batch, s_len, t_att)

        def heads_first(x, width):
            return x[n_p:].reshape(n_seq, n_tok, N_HEADS, width).transpose(0, 2, 1, 3).reshape(
                n_seq, N_HEADS * n_tok, width)

        new_t = lambda x: _keys_on_lanes(x[n_p:], n_seq, n_tok)
        addmask = _samp_select(page_table, heads_first(iq, IDX_DIM), heads_first(iw, 1), new_t(ik),
                               t_a_idx, i, g_pages)
        oa_s = _samp_dsa(page_table, heads_first(aq, HEAD_DIM), addmask, bias_s, new_t(ak), new_t(av),
                         t_a_k, t_a_v, i, g_pages)
        ob_s = _samp_mla(page_table, heads_first(qb, 256), _pad_rows(kb[n_p:].reshape(n_seq, n_tok, -1), 128),
                         cache_b_ckv, t_b_kr, i, g_pages)
        cq_s = cq[n_p:].reshape(n_seq, n_tok, C_KV_HEADS, C_GROUP, HEAD_DIM).transpose(0, 2, 3, 1, 4).reshape(
            n_seq, C_KV_HEADS, C_GROUP * n_tok, HEAD_DIM)
        oc_s = _samp_sb(page_table, cq_s, new_t(ck), new_t(cv), t_c_k, t_c_v, i, g_pages, n_tok)

        def tokens_first(x, width):
            return x.reshape(n_seq, N_HEADS, n_tok, width).transpose(0, 2, 1, 3).reshape(
                n_s, N_HEADS * width).astype(BF16)

        oa = jnp.concatenate([oa_p, tokens_first(oa_s, HEAD_DIM)], axis=0)
        ob = jnp.concatenate([ob_p, tokens_first(ob_s, KV_LORA)], axis=0)
        oc = jnp.concatenate([oc_p, tokens_first(oc_s, HEAD_DIM)], axis=0)

        h1, n2, s_keys = _merge(h, oa, ob, oc, gates, w_br_a[i].astype(BF16), _pack_wb(w_uv[i], w_br_b[i]),
                                w_br_c[i].astype(BF16), w_o[i].astype(BF16), row2(g_ffn[i]),
                                w_pq[i].astype(BF16), sub_k1[i].astype(BF16), sub_k2[i].astype(BF16), tm)
        e_t, g_t = _peer_topk(s_keys, math.gcd(512, n_p + n_s))
        e_tok = e_t.T
        w_t = _peer_u(e_tok, n2.reshape(-1, 8, 128), g_t, _pack_table(peer_u[i]), tb_peer)
        po = _peer_v(e_tok, w_t, _pack_table(peer_v[i]), tb_peer)
        h, y = _ple(h1, po.reshape(-1, D_MODEL), p_all[i], row2(g_ple[i]), w_pg[i].astype(BF16),
                    w_pp[i].astype(BF16), row2(g_final), tm)

    def stack_rows(lst, lead, tail):
        return jnp.stack(lst, axis=0).reshape((depth,) + lead + tail)

    tails = [(HEAD_DIM,), (HEAD_DIM,), (IDX_DIM,), (KV_LORA,), (ROPE_DIM,), (C_KV_HEADS, HEAD_DIM),
             (C_KV_HEADS, HEAD_DIM)]
    out_p = [stack_rows(l, (batch, s_len), t) for l, t in zip(rows_p, tails)]
    out_s = [stack_rows(l, (n_seq, n_tok), t) for l, t in zip(rows_s, tails)]
    return (y[:n_p].reshape(batch, s_len, D_MODEL), y[n_p:].reshape(n_seq, n_tok, D_MODEL), *out_p, *out_s)
```

```python
import functools
import math

import jax
import jax.numpy as jnp
import numpy as np
from jax import lax
from jax.experimental import pallas as pl
from jax.experimental.pallas import tpu as pltpu

F32 = jnp.float32
BF16 = jnp.bfloat16
I32 = jnp.int32

D_MODEL = 1024
HEAD_DIM = 64
EPS = 1e-6
N_HEADS = 8
IDX_DIM = 64
TOPK_MAX = 256
N_BUCKETS = 32
MAX_DISTANCE = 128
Q_LORA = 256
KV_LORA = 128
NOPE_DIM = 64
ROPE_DIM = 32
ROPE_THETA = 10000.0
MLA_SCALE = (NOPE_DIM + ROPE_DIM) ** -0.5
IDX_SCALE = IDX_DIM ** -0.5 * N_HEADS ** -0.5
C_KV_HEADS = 2
C_GROUP = 4
N_KEYS = 128
PEER_TOPK = 16
PEER_PAIRS = N_HEADS * PEER_TOPK
TAB_ROWS = 4
PLE_DIM = 256

NEG = -1e30
INT_MIN = -(2 ** 31)
VMEM_LIMIT = 56 * 1024 * 1024

C_AQ, C_AKV, C_IQ, C_IKW, C_BQ, C_BKV, C_BKR, C_CQ, C_CK, C_CV, C_GATE, C_END = (
    0, 512, 640, 1152, 1280, 1536, 1664, 1792, 2304, 2432, 2560, 5632)

NT = (((1,), (1,)), ((), ()))


def _cparams(sem):
    return pltpu.CompilerParams(dimension_semantics=sem, vmem_limit_bytes=VMEM_LIMIT)


def _rms(x, g):
    return x * lax.rsqrt(jnp.mean(x * x, axis=-1, keepdims=True) + EPS) * g


def _dot(a, b):
    return jnp.dot(a, b, preferred_element_type=F32)


def _dot_nt(a, b):
    return lax.dot_general(a, b, NT, preferred_element_type=F32)


def _project_kernel(x_ref, gmix_ref, w_ref, gq_ref, wq2_ref, gkv_ref, rc_ref, rsa_ref, rsb_ref,
                    aq_ref, ak_ref, av_ref, iq_ref, ik_ref, iw_ref, qb_ref, ckv_ref, kr_ref, kb_ref,
                    cq_ref, ck_ref, cv_ref, gate_ref):
    n = _rms(x_ref[...], gmix_ref[...]).astype(BF16)

    def seg(a, b):
        return _dot(n, w_ref[:, a:b])

    aq_ref[...] = (seg(C_AQ, C_AKV) * HEAD_DIM ** -0.5).astype(BF16)
    t = seg(C_AKV, C_IQ)
    ak_ref[...] = t[:, :HEAD_DIM]
    av_ref[...] = t[:, HEAD_DIM:]
    iq_ref[...] = seg(C_IQ, C_IKW).astype(BF16)
    t = seg(C_IKW, C_BQ)
    ik_ref[...] = t[:, :IDX_DIM]
    iw_ref[...] = t[:, IDX_DIM:IDX_DIM + N_HEADS] * IDX_SCALE

    rc, rsa, rsb = rc_ref[...], rsa_ref[...], rsb_ref[...]

    def rot(r):
        return r * rc + pltpu.roll(r, 16, 1) * rsa + pltpu.roll(r, 112, 1) * rsb

    nq = _rms(seg(C_BQ, C_BKV), gq_ref[...]).astype(BF16)
    for h in range(N_HEADS):
        z = _dot(nq, wq2_ref[:, h * 256:(h + 1) * 256])
        qb_ref[:, h * 256:h * 256 + 128] = (z[:, :128] * MLA_SCALE).astype(BF16)
        qb_ref[:, h * 256 + 128:(h + 1) * 256] = (rot(z[:, 128:]) * MLA_SCALE).astype(BF16)
    c = _rms(seg(C_BKV, C_BKR), gkv_ref[...])
    ckv_ref[...] = c
    kr = rot(seg(C_BKR, C_CQ))
    kr_ref[...] = kr[:, :ROPE_DIM]
    kb_ref[:, :128] = c.astype(BF16)
    kb_ref[:, 128:] = kr.astype(BF16)
    cq_ref[...] = (seg(C_CQ, C_CK) * HEAD_DIM ** -0.5).astype(BF16)
    ck_ref[...] = seg(C_CK, C_CV)
    cv_ref[...] = seg(C_CV, C_GATE)
    gate_ref[...] = jax.nn.sigmoid(seg(C_GATE, C_END)).astype(BF16)


def _project(h, gmix, w_main, gq, wq2, gkv, rope_tabs, n_prompt_blocks, rope_period, tm):
    n_tok = h.shape[0]
    rc, rsa, rsb = rope_tabs

    def rope_map(i):
        return (jnp.where(i < n_prompt_blocks, i % rope_period, rope_period), 0)

    row = lambda w: pl.BlockSpec((tm, w), lambda i: (i, 0))
    full = lambda a: pl.BlockSpec(a.shape, lambda i: (0,) * a.ndim)
    widths = [(512, BF16), (64, F32), (64, F32), (512, BF16), (64, F32), (8, F32), (2048, BF16),
              (128, F32), (32, F32), (256, BF16), (512, BF16), (128, F32), (128, F32), (3072, BF16)]
    return pl.pallas_call(
        _project_kernel,
        grid=(n_tok // tm,),
        in_specs=[row(D_MODEL), full(gmix), full(w_main), full(gq), full(wq2), full(gkv),
                  pl.BlockSpec((tm, 128), rope_map), pl.BlockSpec((tm, 128), rope_map),
                  pl.BlockSpec((tm, 128), rope_map)],
        out_specs=[row(w) for w, _ in widths],
        out_shape=[jax.ShapeDtypeStruct((n_tok, w), d) for w, d in widths],
        compiler_params=_cparams(("parallel",)),
        name="project",
    )(h, gmix, w_main, gq, wq2, gkv, rc, rsa, rsb)


def _sortable(x):
    b = pltpu.bitcast(x, I32)
    k = jnp.where(b >= 0, b, b ^ 0x7FFFFFFF)
    return jnp.where(b == INT_MIN, 0, k)


def _topk_threshold(count, p_ref, k_sel, idx_bits):
    kf = float(k_sel)
    t0 = jnp.where(count(lambda k, i: k >= 0) >= kf, 0, INT_MIN).astype(I32)

    def bis(i, t):
        cand = t + jnp.left_shift(jnp.int32(1), 30 - i)
        return jnp.where(count(lambda k, i_: k >= cand) >= kf, cand, t)

    thr = lax.fori_loop(0, 31, bis, t0)
    need = kf - count(lambda k, i: k > thr)
    excess = jnp.logical_and(count(lambda k, i: k == thr) > need, thr != INT_MIN)
    p_ref[...] = jnp.full(p_ref.shape, 2 ** idx_bits, I32)

    @pl.when(jnp.max(excess.astype(F32)) > 0.0)
    def _():
        def bis_idx(i, p):
            cand = p + jnp.left_shift(jnp.int32(1), idx_bits - 1 - i)
            c = count(lambda k, i_: jnp.logical_and(k == thr, i_ < cand))
            return jnp.where(c < need, cand, p)

        p = lax.fori_loop(0, idx_bits, bis_idx, jnp.zeros(p_ref.shape, I32))
        p_ref[...] = jnp.where(excess, p, 2 ** idx_bits)

    return thr, p_ref[...]


def _selected(key, idx, thr, bound):
    return jnp.logical_or(key > thr, jnp.logical_and(key == thr, idx <= bound))


def _dsa_select_kernel(iq_ref, iw_ref, ik_ref, mask_ref, key_scr, p_scr, *, k_sel, idx_bits):
    tq, s_len = mask_ref.shape
    qi = pl.program_id(1)
    ik = ik_ref[...].astype(BF16)
    iq = iq_ref[...]
    iw = iw_ref[...]
    sc = jnp.zeros((tq, s_len), F32)
    for h in range(N_HEADS):
        s = _dot_nt(iq[:, h * IDX_DIM:(h + 1) * IDX_DIM], ik)
        sc = sc + iw[:, h:h + 1] * jnp.maximum(s, 0.0)
    row = qi * tq + lax.broadcasted_iota(I32, (tq, s_len), 0)
    col = lax.broadcasted_iota(I32, (tq, s_len), 1)
    causal = col <= row
    key = jnp.where(causal, _sortable(sc), INT_MIN)
    n_chunks = s_len // 128
    for c in range(n_chunks):
        key_scr[c] = key[:, c * 128:(c + 1) * 128]
    lane = lax.broadcasted_iota(I32, (tq, 128), 1)
    live = (qi * tq + tq + 127) // 128

    def count(pred):
        acc = lax.fori_loop(0, live, lambda c, a: a + pred(key_scr[c], c * 128 + lane).astype(F32),
                            jnp.zeros((tq, 128), F32))
        return jnp.sum(acc, axis=1, keepdims=True)

    thr, bound = _topk_threshold(count, p_scr, k_sel, idx_bits)
    mask_ref[...] = jnp.where(jnp.logical_and(_selected(key, col, thr, bound), causal), 1.0, 0.0).astype(BF16)


def _dsa_select(iq, iw, ik, batch, s_len, tq):
    nq = s_len // tq
    k_sel = min(TOPK_MAX, s_len // 4)
    idx_bits = max(1, (s_len - 1).bit_length())
    return pl.pallas_call(
        functools.partial(_dsa_select_kernel, k_sel=k_sel, idx_bits=idx_bits),
        grid=(batch, nq),
        in_specs=[pl.BlockSpec((tq, 512), lambda b, q: (b * nq + q, 0)),
                  pl.BlockSpec((tq, N_HEADS), lambda b, q: (b * nq + q, 0)),
                  pl.BlockSpec((s_len, IDX_DIM), lambda b, q: (b, 0))],
        out_specs=pl.BlockSpec((tq, s_len), lambda b, q: (b * nq + q, 0)),
        out_shape=jax.ShapeDtypeStruct((batch * s_len, s_len), BF16),
        scratch_shapes=[pltpu.VMEM((s_len // 128, tq, 128), I32), pltpu.VMEM((tq, 1), I32)],
        compiler_params=_cparams(("parallel", "parallel")),
        name="dsa_select",
    )(iq, iw, ik)


def _softmax_update(s, m_old, l_old, acc_old, pv):
    m_new = jnp.maximum(m_old, jnp.max(s, axis=1, keepdims=True))
    alpha = jnp.exp(m_old - m_new)
    p = jnp.exp(s - pltpu.repeat(m_new, s.shape[1] // 128, axis=1))
    l_new = alpha * l_old + jnp.sum(p, axis=1, keepdims=True)
    return m_new, l_new, alpha[:, :acc_old.shape[1]] * acc_old + pv(p.astype(BF16))


def _heads_update(m_scr, l_scr, acc_scr, logits, pv):
    state = [(m_scr[h], l_scr[h], acc_scr[h]) for h in range(N_HEADS)]
    new = [_softmax_update(logits(h), *state[h], pv) for h in range(N_HEADS)]
    for h in range(N_HEADS):
        m_scr[h], l_scr[h], acc_scr[h] = new[h]


def _dsa_attn_kernel(q_ref, k_ref, v_ref, mask_ref, bias_ref, o_ref, qh_scr, m_scr, l_scr, acc_scr):
    qi, ki = pl.program_id(1), pl.program_id(2)
    nk = pl.num_programs(2)

    @pl.when(ki == 0)
    def _():
        m_scr[...] = jnp.full(m_scr.shape, NEG, F32)
        l_scr[...] = jnp.zeros(l_scr.shape, F32)
        acc_scr[...] = jnp.zeros(acc_scr.shape, F32)
        for h in range(N_HEADS):
            qh_scr[h] = q_ref[:, h * HEAD_DIM:(h + 1) * HEAD_DIM]

    @pl.when(ki <= qi)
    def _():
        k = k_ref[...].astype(BF16)
        v = v_ref[...].astype(BF16)
        msk = mask_ref[...] > 0
        tix = jnp.minimum(qi - ki, 2)
        _heads_update(m_scr, l_scr, acc_scr,
                      lambda h: jnp.where(msk, _dot_nt(qh_scr[h], k) + bias_ref[tix, h], NEG),
                      lambda p: _dot(p, v))

    @pl.when(ki == nk - 1)
    def _():
        for h in range(N_HEADS):
            o_ref[:, h * HEAD_DIM:(h + 1) * HEAD_DIM] = (acc_scr[h] / l_scr[h][:, :HEAD_DIM]).astype(BF16)


def _dsa_attn(aq, ak, av, mask, bias_tiles, batch, s_len, t):
    nb = s_len // t
    return pl.pallas_call(
        _dsa_attn_kernel,
        grid=(batch, nb, nb),
        in_specs=[pl.BlockSpec((t, 512), lambda b, q, k: (b * nb + q, 0)),
                  pl.BlockSpec((t, HEAD_DIM), lambda b, q, k: (b * nb + jnp.minimum(k, q), 0)),
                  pl.BlockSpec((t, HEAD_DIM), lambda b, q, k: (b * nb + jnp.minimum(k, q), 0)),
                  pl.BlockSpec((t, t), lambda b, q, k: (b * nb + q, jnp.minimum(k, q))),
                  pl.BlockSpec(bias_tiles.shape, lambda b, q, k: (0, 0, 0, 0))],
        out_specs=pl.BlockSpec((t, 512), lambda b, q, k: (b * nb + q, 0)),
        out_shape=jax.ShapeDtypeStruct((batch * s_len, 512), BF16),
        scratch_shapes=[pltpu.VMEM((N_HEADS, t, HEAD_DIM), BF16), pltpu.VMEM((N_HEADS, t, 128), F32),
                        pltpu.VMEM((N_HEADS, t, 128), F32), pltpu.VMEM((N_HEADS, t, HEAD_DIM), F32)],
        compiler_params=_cparams(("parallel", "parallel", "arbitrary")),
        name="dsa_attn",
    )(aq, ak, av, mask, bias_tiles)


def _mla_attn_kernel(q_ref, kb_ref, o_ref, m_scr, l_scr, acc_scr):
    qi, ki = pl.program_id(1), pl.program_id(2)
    nk = pl.num_programs(2)
    t = kb_ref.shape[0]

    @pl.when(ki == 0)
    def _():
        m_scr[...] = jnp.full(m_scr.shape, NEG, F32)
        l_scr[...] = jnp.zeros(l_scr.shape, F32)
        acc_scr[...] = jnp.zeros(acc_scr.shape, F32)

    @pl.when(ki <= qi)
    def _():
        kb = kb_ref[...]
        v = kb[:, :KV_LORA]
        row = qi * t + lax.broadcasted_iota(I32, (t, t), 0)
        col = ki * t + lax.broadcasted_iota(I32, (t, t), 1)
        msk = col <= row
        _heads_update(m_scr, l_scr, acc_scr,
                      lambda h: jnp.where(msk, _dot_nt(q_ref[:, h * 256:(h + 1) * 256], kb), NEG),
                      lambda p: _dot(p, v))

    @pl.when(ki == nk - 1)
    def _():
        for h in range(N_HEADS):
            o_ref[:, h * KV_LORA:(h + 1) * KV_LORA] = (acc_scr[h] / l_scr[h]).astype(BF16)


def _mla_attn(qb, kb, batch, s_len, t):
    nb = s_len // t
    return pl.pallas_call(
        _mla_attn_kernel,
        grid=(batch, nb, nb),
        in_specs=[pl.BlockSpec((t, 2048), lambda b, q, k: (b * nb + q, 0)),
                  pl.BlockSpec((t, 256), lambda b, q, k: (b * nb + jnp.minimum(k, q), 0))],
        out_specs=pl.BlockSpec((t, 1024), lambda b, q, k: (b * nb + q, 0)),
        out_shape=jax.ShapeDtypeStruct((batch * s_len, 1024), BF16),
        scratch_shapes=[pltpu.VMEM((N_HEADS, t, 128), F32), pltpu.VMEM((N_HEADS, t, 128), F32),
                        pltpu.VMEM((N_HEADS, t, KV_LORA), F32)],
        compiler_params=_cparams(("parallel", "parallel", "arbitrary")),
        name="mla_attn",
    )(qb, kb)


def _sb_block(z, strict, tri, carry, pv):
    lk = -(jnp.maximum(z, 0.0) + jnp.log(1.0 + jnp.exp(-jnp.abs(z))))
    if strict is not None:
        lk = jnp.where(strict, lk, 0.0)
    hi = lk.astype(BF16)
    lo = (lk - hi.astype(F32)).astype(BF16)
    cs = _dot(hi, tri) + _dot(lo, tri)
    a = jnp.exp(z + cs + pltpu.repeat(carry, z.shape[1] // 128, axis=1))
    if strict is not None:
        a = jnp.where(strict, a, 0.0)
    return pv(a.astype(BF16)), carry + jnp.sum(lk, axis=1, keepdims=True)


def _tri(n):
    return jnp.where(lax.broadcasted_iota(I32, (n, n), 0) >= lax.broadcasted_iota(I32, (n, n), 1),
                     1.0, 0.0).astype(BF16)


def _sb_attn_kernel(q_ref, k_ref, v_ref, o_ref, qh_scr, carry_scr, acc_scr):
    qi, j = pl.program_id(1), pl.program_id(2)
    nk = pl.num_programs(2)
    t = k_ref.shape[0]
    ki = qi - j

    @pl.when(j == 0)
    def _():
        carry_scr[...] = jnp.zeros(carry_scr.shape, F32)
        acc_scr[...] = jnp.zeros(acc_scr.shape, F32)
        for h in range(N_HEADS):
            qh_scr[h] = q_ref[:, h * HEAD_DIM:(h + 1) * HEAD_DIM]

    @pl.when(j <= qi)
    def _():
        k = k_ref[...].astype(BF16)
        v = v_ref[...].astype(BF16)
        row = qi * t + lax.broadcasted_iota(I32, (t, t), 0)
        col = ki * t + lax.broadcasted_iota(I32, (t, t), 1)
        strict = col < row
        tri = _tri(t)
        kn = [k[:, n * HEAD_DIM:(n + 1) * HEAD_DIM] for n in range(C_KV_HEADS)]
        vn = [v[:, n * HEAD_DIM:(n + 1) * HEAD_DIM] for n in range(C_KV_HEADS)]
        state = [(carry_scr[h], acc_scr[h]) for h in range(N_HEADS)]
        new = []
        for h in range(N_HEADS):
            n = h // C_GROUP
            o, c = _sb_block(_dot_nt(qh_scr[h], kn[n]), strict, tri, state[h][0], lambda a, n=n: _dot(a, vn[n]))
            new.append((c, state[h][1] + o))
        for h in range(N_HEADS):
            carry_scr[h], acc_scr[h] = new[h]

    @pl.when(j == nk - 1)
    def _():
        for h in range(N_HEADS):
            o_ref[:, h * HEAD_DIM:(h + 1) * HEAD_DIM] = acc_scr[h].astype(BF16)


def _sb_attn(cq, ck, cv, batch, s_len, t):
    nb = s_len // t
    kmap = lambda b, q, j: (b * nb + jnp.maximum(q - j, 0), 0)
    return pl.pallas_call(
        _sb_attn_kernel,
        grid=(batch, nb, nb),
        in_specs=[pl.BlockSpec((t, 512), lambda b, q, j: (b * nb + q, 0)),
                  pl.BlockSpec((t, 128), kmap), pl.BlockSpec((t, 128), kmap)],
        out_specs=pl.BlockSpec((t, 512), lambda b, q, j: (b * nb + q, 0)),
        out_shape=jax.ShapeDtypeStruct((batch * s_len, 512), BF16),
        scratch_shapes=[pltpu.VMEM((N_HEADS, t, HEAD_DIM), BF16), pltpu.VMEM((N_HEADS, t, 128), F32),
                        pltpu.VMEM((N_HEADS, t, HEAD_DIM), F32)],
        compiler_params=_cparams(("parallel", "parallel", "arbitrary")),
        name="sb_attn",
    )(cq, ck, cv)


def _page_specs(cache, layer, g_pages, n_pages, page_of):
    page, width = cache.shape[2], cache.shape[3]

    def spec(g):
        return pl.BlockSpec((None, None, page, width),
                            lambda b, j, pt: (layer, pt[b * n_pages + page_of(j, g)], 0, 0))

    return [spec(g) for g in range(g_pages)]


def _samp_select_kernel(pt_ref, iq_ref, iw_ref, iknew_ref, *rest, g_pages, n_pages, k_sel, idx_bits):
    pages = rest[:g_pages]
    out_ref, sc_scr, key_scr, p_scr = rest[g_pages:]
    j = pl.program_id(1)
    ns = pl.num_programs(1)
    q = iq_ref[...]
    w = iw_ref[...]
    n_tok = out_ref.shape[0]

    def score(kblk_t):
        s = jnp.maximum(_dot(q, kblk_t.astype(BF16)), 0.0) * w
        tot = s[0:n_tok]
        for h in range(1, N_HEADS):
            tot = tot + s[h * n_tok:(h + 1) * n_tok]
        return tot

    for g in range(g_pages):
        sc_scr[j * g_pages + g] = score(pages[g][...])

    @pl.when(j == ns - 1)
    def _():
        sc_scr[n_pages] = score(iknew_ref[...])
        shape = sc_scr.shape
        page_i = lax.broadcasted_iota(I32, shape, 0)
        row = lax.broadcasted_iota(I32, shape, 1)
        lane = lax.broadcasted_iota(I32, shape, 2)
        causal = jnp.logical_or(page_i < n_pages, lane <= row)
        key_scr[...] = jnp.where(causal, _sortable(sc_scr[...]), INT_MIN)
        idx = page_i * 128 + lane

        def count(pred):
            m = pred(key_scr[...], idx).astype(F32)
            return jnp.sum(jnp.sum(m, axis=0), axis=1, keepdims=True)[None]

        thr, bound = _topk_threshold(count, p_scr, k_sel, idx_bits)
        sel = _selected(key_scr[...], idx, thr, bound)
        key_scr[...] = jnp.where(jnp.logical_and(sel, causal), 1, 0)
        for p in range(n_pages + 1):
            out_ref[:, p * 128:(p + 1) * 128] = jnp.where(key_scr[p] > 0, 0.0, NEG)


def _samp_select(page_table, iq_s, iw_s, ik_new, cache_idx, layer, g_pages):
    n_seq, n_tok = ik_new.shape[0], iq_s.shape[1] // N_HEADS
    n_pages = page_table.shape[1]
    ns = n_pages // g_pages
    n_keys = n_pages * 128 + n_tok
    k_sel = min(TOPK_MAX, n_keys // 4)
    idx_bits = ((n_pages + 1) * 128 - 1).bit_length()
    grid_spec = pltpu.PrefetchScalarGridSpec(
        num_scalar_prefetch=1,
        grid=(n_seq, ns),
        in_specs=[pl.BlockSpec((None, N_HEADS * n_tok, IDX_DIM), lambda b, j, pt: (b, 0, 0)),
                  pl.BlockSpec((None, N_HEADS * n_tok, 1), lambda b, j, pt: (b, 0, 0)),
                  pl.BlockSpec((None, IDX_DIM, 128), lambda b, j, pt: (b, 0, 0))]
        + _page_specs(cache_idx, layer, g_pages, n_pages, lambda j, g: j * g_pages + g),
        out_specs=pl.BlockSpec((None, n_tok, (n_pages + 1) * 128), lambda b, j, pt: (b, 0, 0)),
        scratch_shapes=[pltpu.VMEM((n_pages + 1, n_tok, 128), F32),
                        pltpu.VMEM((n_pages + 1, n_tok, 128), I32),
                        pltpu.VMEM((1, n_tok, 1), I32)],
    )
    return pl.pallas_call(
        functools.partial(_samp_select_kernel, g_pages=g_pages, n_pages=n_pages, k_sel=k_sel,
                          idx_bits=idx_bits),
        grid_spec=grid_spec,
        out_shape=jax.ShapeDtypeStruct((n_seq, n_tok, (n_pages + 1) * 128), F32),
        compiler_params=_cparams(("parallel", "arbitrary")),
        name="samp_select",
    )(page_table.reshape(-1), iq_s, iw_s, ik_new, *([cache_idx] * g_pages))


def _state_update(s, m_scr, l_scr, acc_scr, pv):
    m_scr[...], l_scr[...], acc_scr[...] = _softmax_update(s, m_scr[...], l_scr[...], acc_scr[...], pv)


def _init_softmax_state(m_scr, l_scr, acc_scr):
    m_scr[...] = jnp.full(m_scr.shape, NEG, F32)
    l_scr[...] = jnp.zeros(l_scr.shape, F32)
    acc_scr[...] = jnp.zeros(acc_scr.shape, F32)


def _lanes(x, g):
    return x[:, g * 128:(g + 1) * 128]


def _samp_dsa_kernel(pt_ref, q_ref, am_ref, amnew_ref, bias_ref, biasnew_ref, knew_ref, vnew_ref, *rest,
                     g_pages):
    kp = rest[:g_pages]
    vp = rest[g_pages:2 * g_pages]
    o_ref, m_scr, l_scr, acc_scr = rest[2 * g_pages:]
    j = pl.program_id(1)
    ns = pl.num_programs(1)
    q = q_ref[...]

    @pl.when(j == 0)
    def _():
        _init_softmax_state(m_scr, l_scr, acc_scr)

    s = jnp.concatenate([_dot(q, kp[g][...].astype(BF16)) for g in range(g_pages)], axis=1)
    s = s + bias_ref[...] + jnp.concatenate([am_ref[...]] * N_HEADS, axis=0)

    def pv(p):
        out = _dot_nt(_lanes(p, 0), vp[0][...].astype(BF16))
        for g in range(1, g_pages):
            out = out + _dot_nt(_lanes(p, g), vp[g][...].astype(BF16))
        return out

    _state_update(s, m_scr, l_scr, acc_scr, pv)

    @pl.when(j == ns - 1)
    def _():
        s_new = (_dot(q, knew_ref[...].astype(BF16)) + biasnew_ref[...]
                 + jnp.concatenate([amnew_ref[...]] * N_HEADS, axis=0))
        _state_update(s_new, m_scr, l_scr, acc_scr, lambda p: _dot_nt(p, vnew_ref[...].astype(BF16)))
        o_ref[...] = acc_scr[...] / l_scr[...][:, :HEAD_DIM]


def _samp_dsa(page_table, q_s, addmask, bias_s, k_new, v_new, cache_k, cache_v, layer, g_pages):
    n_seq, rows = q_s.shape[0], q_s.shape[1]
    n_pages = page_table.shape[1]
    n_tok = rows // N_HEADS
    wide = g_pages * 128
    page_of = lambda j, g: j * g_pages + g
    grid_spec = pltpu.PrefetchScalarGridSpec(
        num_scalar_prefetch=1,
        grid=(n_seq, n_pages // g_pages),
        in_specs=[pl.BlockSpec((None, rows, HEAD_DIM), lambda b, j, pt: (b, 0, 0)),
                  pl.BlockSpec((None, n_tok, wide), lambda b, j, pt: (b, 0, j)),
                  pl.BlockSpec((None, n_tok, 128), lambda b, j, pt: (b, 0, n_pages)),
                  pl.BlockSpec((rows, wide), lambda b, j, pt: (0, j)),
                  pl.BlockSpec((rows, 128), lambda b, j, pt: (0, n_pages)),
                  pl.BlockSpec((None, HEAD_DIM, 128), lambda b, j, pt: (b, 0, 0)),
                  pl.BlockSpec((None, HEAD_DIM, 128), lambda b, j, pt: (b, 0, 0))]
        + _page_specs(cache_k, layer, g_pages, n_pages, page_of)
        + _page_specs(cache_v, layer, g_pages, n_pages, page_of),
        out_specs=pl.BlockSpec((None, rows, HEAD_DIM), lambda b, j, pt: (b, 0, 0)),
        scratch_shapes=[pltpu.VMEM((rows, 128), F32), pltpu.VMEM((rows, 128), F32),
                        pltpu.VMEM((rows, HEAD_DIM), F32)],
    )
    return pl.pallas_call(
        functools.partial(_samp_dsa_kernel, g_pages=g_pages),
        grid_spec=grid_spec,
        out_shape=jax.ShapeDtypeStruct((n_seq, rows, HEAD_DIM), F32),
        compiler_params=_cparams(("parallel", "arbitrary")),
        name="samp_dsa",
    )(page_table.reshape(-1), q_s, addmask, addmask, bias_s, bias_s, k_new, v_new,
      *([cache_k] * g_pages), *([cache_v] * g_pages))


def _samp_mla_kernel(pt_ref, q_ref, kbnew_ref, *rest, g_pages, n_tok):
    cp = rest[:g_pages]
    rp = rest[g_pages:2 * g_pages]
    o_ref, m_scr, l_scr, acc_scr = rest[2 * g_pages:]
    j = pl.program_id(1)
    ns = pl.num_programs(1)
    q = q_ref[...]

    @pl.when(j == 0)
    def _():
        _init_softmax_state(m_scr, l_scr, acc_scr)

    cs = [cp[g][...].astype(BF16) for g in range(g_pages)]
    q_lat, q_rope = q[:, :KV_LORA], q[:, KV_LORA:KV_LORA + ROPE_DIM]
    s = jnp.concatenate([_dot_nt(q_lat, cs[g]) + _dot(q_rope, rp[g][...].astype(BF16))
                         for g in range(g_pages)], axis=1)

    def pv(p):
        out = _dot(_lanes(p, 0), cs[0])
        for g in range(1, g_pages):
            out = out + _dot(_lanes(p, g), cs[g])
        return out

    _state_update(s, m_scr, l_scr, acc_scr, pv)

    @pl.when(j == ns - 1)
    def _():
        kb = kbnew_ref[...]
        s_new = _dot_nt(q, kb)
        row = lax.broadcasted_iota(I32, s_new.shape, 0) % n_tok
        col = lax.broadcasted_iota(I32, s_new.shape, 1)
        _state_update(jnp.where(col <= row, s_new, NEG), m_scr, l_scr, acc_scr,
                      lambda p: _dot(p, kb[:, :KV_LORA]))
        o_ref[...] = acc_scr[...] / l_scr[...]


def _samp_mla(page_table, q_s, kb_new, cache_ckv, cache_kr, layer, g_pages):
    n_seq, rows = q_s.shape[0], q_s.shape[1]
    n_pages = page_table.shape[1]
    page_of = lambda j, g: j * g_pages + g
    grid_spec = pltpu.PrefetchScalarGridSpec(
        num_scalar_prefetch=1,
        grid=(n_seq, n_pages // g_pages),
        in_specs=[pl.BlockSpec((None, rows, 256), lambda b, j, pt: (b, 0, 0)),
                  pl.BlockSpec((None, 128, 256), lambda b, j, pt: (b, 0, 0))]
        + _page_specs(cache_ckv, layer, g_pages, n_pages, page_of)
        + _page_specs(cache_kr, layer, g_pages, n_pages, page_of),
        out_specs=pl.BlockSpec((None, rows, KV_LORA), lambda b, j, pt: (b, 0, 0)),
        scratch_shapes=[pltpu.VMEM((rows, 128), F32), pltpu.VMEM((rows, 128), F32),
                        pltpu.VMEM((rows, KV_LORA), F32)],
    )
    return pl.pallas_call(
        functools.partial(_samp_mla_kernel, g_pages=g_pages, n_tok=rows // N_HEADS),
        grid_spec=grid_spec,
        out_shape=jax.ShapeDtypeStruct((n_seq, rows, KV_LORA), F32),
        compiler_params=_cparams(("parallel", "arbitrary")),
        name="samp_mla",
    )(page_table.reshape(-1), q_s, kb_new, *([cache_ckv] * g_pages), *([cache_kr] * g_pages))


def _samp_sb_kernel(pt_ref, q_ref, knew_ref, vnew_ref, *rest, g_pages, n_tok):
    kp = rest[:g_pages]
    vp = rest[g_pages:2 * g_pages]
    o_ref, carry_scr, acc_scr = rest[2 * g_pages:]
    j = pl.program_id(1)
    ns = pl.num_programs(1)
    rows = q_ref.shape[1]
    tri = _tri(128)
    head = lambda x, n: x[n * HEAD_DIM:(n + 1) * HEAD_DIM, :].astype(BF16)

    @pl.when(j == 0)
    def _():
        carry_scr[...] = jnp.zeros(carry_scr.shape, F32)
        acc_scr[...] = jnp.zeros(acc_scr.shape, F32)
        row = lax.broadcasted_iota(I32, (rows, 128), 0) % n_tok
        col = lax.broadcasted_iota(I32, (rows, 128), 1)
        for n in range(C_KV_HEADS):
            o, c = _sb_block(_dot(q_ref[n], head(knew_ref[...], n)), col < row, tri, carry_scr[n],
                             lambda a, n=n: _dot_nt(a, head(vnew_ref[...], n)))
            acc_scr[n] = o
            carry_scr[n] = c

    order = list(reversed(range(g_pages)))
    for n in range(C_KV_HEADS):
        z = [_dot(q_ref[n], head(kp[g][...], n)) for g in order]
        lk = [-(jnp.maximum(x, 0.0) + jnp.log(1.0 + jnp.exp(-jnp.abs(x)))) for x in z]
        lk_all = jnp.concatenate(lk, axis=0)
        hi = lk_all.astype(BF16)
        lo = (lk_all - hi.astype(F32)).astype(BF16)
        cs = _dot(hi, tri) + _dot(lo, tri)
        carry = carry_scr[n]
        out = acc_scr[n]
        for i, g in enumerate(order):
            a = jnp.exp(z[i] + cs[i * rows:(i + 1) * rows] + carry)
            out = out + _dot_nt(a.astype(BF16), head(vp[g][...], n))
            carry = carry + jnp.sum(lk[i], axis=1, keepdims=True)
        acc_scr[n] = out
        carry_scr[n] = carry

    @pl.when(j == ns - 1)
    def _():
        o_ref[...] = acc_scr[...]


def _samp_sb(page_table, q_s, k_new, v_new, cache_k, cache_v, layer, g_pages, n_tok):
    n_seq, rows = q_s.shape[0], q_s.shape[2]
    n_pages = page_table.shape[1]
    ns = n_pages // g_pages
    page_of = lambda j, g: (ns - 1 - j) * g_pages + g
    grid_spec = pltpu.PrefetchScalarGridSpec(
        num_scalar_prefetch=1,
        grid=(n_seq, ns),
        in_specs=[pl.BlockSpec((None, C_KV_HEADS, rows, HEAD_DIM), lambda b, j, pt: (b, 0, 0, 0)),
                  pl.BlockSpec((None, 128, 128), lambda b, j, pt: (b, 0, 0)),
                  pl.BlockSpec((None, 128, 128), lambda b, j, pt: (b, 0, 0))]
        + _page_specs(cache_k, layer, g_pages, n_pages, page_of)
        + _page_specs(cache_v, layer, g_pages, n_pages, page_of),
        out_specs=pl.BlockSpec((None, C_KV_HEADS, rows, HEAD_DIM), lambda b, j, pt: (b, 0, 0, 0)),
        scratch_shapes=[pltpu.VMEM((C_KV_HEADS, rows, 128), F32),
                        pltpu.VMEM((C_KV_HEADS, rows, HEAD_DIM), F32)],
    )
    return pl.pallas_call(
        functools.partial(_samp_sb_kernel, g_pages=g_pages, n_tok=n_tok),
        grid_spec=grid_spec,
        out_shape=jax.ShapeDtypeStruct((n_seq, C_KV_HEADS, rows, HEAD_DIM), F32),
        compiler_params=_cparams(("parallel", "arbitrary")),
        name="samp_sb",
    )(page_table.reshape(-1), q_s, k_new, v_new, *([cache_k] * g_pages), *([cache_v] * g_pages))


def _merge_kernel(h_ref, oa_ref, ob_ref, oc_ref, gate_ref, wa_ref, wb_ref, wc_ref, wo_ref, gffn_ref,
                  wpq_ref, sk1_ref, sk2_ref, h1_ref, n2_ref, s_ref):
    g = gate_ref[...].astype(F32)
    m = (g[:, :D_MODEL] * _dot(oa_ref[...], wa_ref[...])
         + g[:, D_MODEL:2 * D_MODEL] * _dot(ob_ref[...], wb_ref[...])
         + g[:, 2 * D_MODEL:] * _dot(oc_ref[...], wc_ref[...]))
    h1 = h_ref[...] + _dot(m.astype(BF16), wo_ref[...])
    h1_ref[...] = h1
    n2 = _rms(h1, gffn_ref[...])
    n2_ref[...] = n2
    q = _dot(n2.astype(BF16), wpq_ref[...])
    for h in range(N_HEADS):
        for j, sk in enumerate((sk1_ref, sk2_ref)):
            qh = q[:, h * 256 + j * 128:h * 256 + (j + 1) * 128].astype(BF16)
            s_ref[j, h] = _dot_nt(sk[...], qh)


def _merge(h, oa, ob, oc, gates, wa, wb, wc, wo, gffn, wpq, sk1, sk2, tm):
    n_tok = h.shape[0]
    row = lambda w: pl.BlockSpec((tm, w), lambda i: (i, 0))
    full = lambda a: pl.BlockSpec(a.shape, lambda i: (0,) * a.ndim)
    return pl.pallas_call(
        _merge_kernel,
        grid=(n_tok // tm,),
        in_specs=[row(D_MODEL), row(512), row(1024), row(512), row(3072), full(wa), full(wb), full(wc),
                  full(wo), full(gffn), full(wpq), full(sk1), full(sk2)],
        out_specs=[row(D_MODEL), row(D_MODEL),
                   pl.BlockSpec((2, N_HEADS, N_KEYS, tm), lambda i: (0, 0, 0, i))],
        out_shape=[jax.ShapeDtypeStruct((n_tok, D_MODEL), F32), jax.ShapeDtypeStruct((n_tok, D_MODEL), F32),
                   jax.ShapeDtypeStruct((2, N_HEADS, N_KEYS, n_tok), F32)],
        compiler_params=_cparams(("parallel",)),
        name="merge",
    )(h, oa, ob, oc, gates, wa, wb, wc, wo, gffn, wpq, sk1, sk2)


def _extract16(s, ids):
    vals, idxs = [], []
    for _ in range(PEER_TOPK):
        m = jnp.max(s, axis=0, keepdims=True)
        idx = jnp.min(jnp.where(s == m, ids, 2 ** 30), axis=0, keepdims=True)
        vals.append(m)
        idxs.append(idx)
        s = jnp.where(ids == idx, -jnp.inf, s)
    return vals, idxs


def _peer_topk_kernel(s_ref, e_ref, g_ref):
    cols = s_ref.shape[-1]
    kio = lax.broadcasted_iota(I32, (N_KEYS, 128), 0)
    for c in range(cols // 128):
        sl = slice(c * 128, (c + 1) * 128)
        v1, i1 = _extract16(s_ref[0, 0, :, sl], kio)
        v2, i2 = _extract16(s_ref[1, 0, :, sl], kio)
        v1a, i1a = jnp.concatenate(v1, axis=0), jnp.concatenate(i1, axis=0)
        v2a, i2a = jnp.concatenate(v2, axis=0), jnp.concatenate(i2, axis=0)
        cv = [v1[0] + v2a] + [v1[a] + v2a[:8] for a in range(1, 8)] + [v1a[8:] + v2[0]]
        ci = [i1[0] * N_KEYS + i2a] + [i1[a] * N_KEYS + i2a[:8] for a in range(1, 8)] + [i1a[8:] * N_KEYS + i2[0]]
        sc, e = _extract16(jnp.concatenate(cv, axis=0), jnp.concatenate(ci, axis=0))
        sc = jnp.concatenate(sc, axis=0)
        p = jnp.exp(sc - sc[0:1])
        e_ref[:, sl] = jnp.concatenate(e, axis=0)
        g_ref[:, sl] = p / jnp.sum(p, axis=0, keepdims=True)


def _peer_topk(s, tc):
    n_tok = s.shape[-1]
    return pl.pallas_call(
        _peer_topk_kernel,
        grid=(N_HEADS, n_tok // tc),
        in_specs=[pl.BlockSpec((2, 1, N_KEYS, tc), lambda h, i: (0, h, 0, i))],
        out_specs=[pl.BlockSpec((PEER_TOPK, tc), lambda h, i: (h, i)),
                   pl.BlockSpec((PEER_TOPK, tc), lambda h, i: (h, i))],
        out_shape=[jax.ShapeDtypeStruct((PEER_PAIRS, n_tok), I32),
                   jax.ShapeDtypeStruct((PEER_PAIRS, n_tok), F32)],
        compiler_params=_cparams(("parallel", "parallel")),
        name="peer_topk",
    )(s)


def _unpack(w):
    hi = pltpu.bitcast(jnp.bitwise_and(w, jnp.uint32(0xFFFF0000)), F32)
    lo = pltpu.bitcast(jnp.left_shift(w, jnp.uint32(16)), F32)
    return hi, lo


def _peer_u_kernel(e_ref, x_ref, g_ref, tab_ref, w_ref, p0_scr, p1_scr):
    tb = x_ref.shape[0]
    lane = lax.broadcasted_iota(I32, (PEER_PAIRS, tb), 1)

    def gather(t, p_scr):
        x = x_ref[t]
        xh, xl = x[0:4], x[4:8]
        for j in range(PEER_PAIRS):
            hi, lo = _unpack(tab_ref[pl.ds(pl.multiple_of(e_ref[t, j], TAB_ROWS), TAB_ROWS), :])
            p_scr[4 * j:4 * j + 4, :] = hi * xh + lo * xl

    def reduce(p_scr, t, acts):
        r = (p_scr[pl.ds(0, PEER_PAIRS, stride=4), :] + p_scr[pl.ds(1, PEER_PAIRS, stride=4), :]
             + p_scr[pl.ds(2, PEER_PAIRS, stride=4), :] + p_scr[pl.ds(3, PEER_PAIRS, stride=4), :])
        return jnp.where(lane == t, jnp.sum(r, axis=1, keepdims=True), acts)

    def two_tokens(i, acts):
        gather(2 * i, p0_scr)
        acts = reduce(p1_scr, 2 * i - 1, acts)
        gather(2 * i + 1, p1_scr)
        return reduce(p0_scr, 2 * i, acts)

    p1_scr[...] = jnp.zeros(p1_scr.shape, F32)
    a = lax.fori_loop(0, tb // 2, two_tokens, jnp.zeros((PEER_PAIRS, tb), F32))
    a = reduce(p1_scr, tb - 1, a)
    w_ref[...] = g_ref[...] * jax.nn.gelu(a)


def _peer_u(e_tok, x3, g_t, tab, tb):
    n_tok = x3.shape[0]
    return pl.pallas_call(
        _peer_u_kernel,
        grid=(n_tok // tb,),
        in_specs=[pl.BlockSpec((tb, PEER_PAIRS), lambda i: (i, 0), memory_space=pltpu.SMEM),
                  pl.BlockSpec((tb, 8, 128), lambda i: (i, 0, 0)),
                  pl.BlockSpec((PEER_PAIRS, tb), lambda i: (0, i)),
                  pl.BlockSpec(memory_space=pltpu.VMEM)],
        out_specs=pl.BlockSpec((PEER_PAIRS, tb), lambda i: (0, i)),
        out_shape=jax.ShapeDtypeStruct((PEER_PAIRS, n_tok), F32),
        scratch_shapes=[pltpu.VMEM((4 * PEER_PAIRS, 128), F32), pltpu.VMEM((4 * PEER_PAIRS, 128), F32)],
        compiler_params=_cparams(("arbitrary",)),
        name="peer_u",
    )(e_tok, x3, g_t, tab)


def _peer_v_kernel(e_ref, w_ref, tab_ref, o_ref, w0_scr, w1_scr):
    tb = o_ref.shape[0]
    lane = lax.broadcasted_iota(I32, (PEER_PAIRS, tb), 1)

    def stage_weights(t, wrep_scr):
        col = jnp.sum(jnp.where(lane == t, w_ref[...], 0.0), axis=1, keepdims=True)
        wrep_scr[...] = jnp.broadcast_to(col, (PEER_PAIRS, 128))

    def gather(t, wrep_scr):
        acc_h = [jnp.zeros((4, 128), F32) for _ in range(4)]
        acc_l = [jnp.zeros((4, 128), F32) for _ in range(4)]
        for j in range(PEER_PAIRS):
            hi, lo = _unpack(tab_ref[pl.ds(pl.multiple_of(e_ref[t, j], TAB_ROWS), TAB_ROWS), :])
            wgt = jnp.broadcast_to(wrep_scr[j:j + 1, :], (4, 128))
            acc_h[j % 4] = acc_h[j % 4] + wgt * hi
            acc_l[j % 4] = acc_l[j % 4] + wgt * lo
        o_ref[t] = jnp.concatenate([(acc_h[0] + acc_h[1]) + (acc_h[2] + acc_h[3]),
                                    (acc_l[0] + acc_l[1]) + (acc_l[2] + acc_l[3])], axis=0)

    def two_tokens(i, carry):
        stage_weights(2 * i + 1, w1_scr)
        gather(2 * i, w0_scr)
        stage_weights(2 * i + 2, w0_scr)
        gather(2 * i + 1, w1_scr)
        return carry

    stage_weights(0, w0_scr)
    lax.fori_loop(0, tb // 2, two_tokens, 0)


def _peer_v(e_tok, w_t, tab, tb):
    n_tok = e_tok.shape[0]
    return pl.pallas_call(
        _peer_v_kernel,
        grid=(n_tok // tb,),
        in_specs=[pl.BlockSpec((tb, PEER_PAIRS), lambda i: (i, 0), memory_space=pltpu.SMEM),
                  pl.BlockSpec((PEER_PAIRS, tb), lambda i: (0, i)),
                  pl.BlockSpec(memory_space=pltpu.VMEM)],
        out_specs=pl.BlockSpec((tb, 8, 128), lambda i: (i, 0, 0)),
        out_shape=jax.ShapeDtypeStruct((n_tok, 8, 128), F32),
        scratch_shapes=[pltpu.VMEM((PEER_PAIRS, 128), F32), pltpu.VMEM((PEER_PAIRS, 128), F32)],
        compiler_params=_cparams(("arbitrary",)),
        name="peer_v",
    )(e_tok, w_t, tab)


def _ple_kernel(h1_ref, po_ref, p_ref, gple_ref, wpg_ref, wpp_ref, gfin_ref, h_ref, y_ref):
    h2 = h1_ref[...] + po_ref[...]
    n = _rms(h2, gple_ref[...]).astype(BF16)
    h3 = h2 + jax.nn.sigmoid(_dot(n, wpg_ref[...])) * _dot(p_ref[...].astype(BF16), wpp_ref[...])
    h_ref[...] = h3
    y_ref[...] = _rms(h3, gfin_ref[...])


def _ple(h1, po, p, gple, wpg, wpp, gfin, tm):
    n_tok = h1.shape[0]
    row = lambda w: pl.BlockSpec((tm, w), lambda i: (i, 0))
    full = lambda a: pl.BlockSpec(a.shape, lambda i: (0,) * a.ndim)
    return pl.pallas_call(
        _ple_kernel,
        grid=(n_tok // tm,),
        in_specs=[row(D_MODEL), row(D_MODEL), row(PLE_DIM), full(gple), full(wpg), full(wpp), full(gfin)],
        out_specs=[row(D_MODEL), row(D_MODEL)],
        out_shape=[jax.ShapeDtypeStruct((n_tok, D_MODEL), F32)] * 2,
        compiler_params=_cparams(("parallel",)),
        name="ple",
    )(h1, po, p, gple, wpg, wpp, gfin)


def _pack_w_in(w):
    z = lambda n: jnp.zeros((w.shape[0], n), w.dtype)
    o = np.cumsum([0, 512, 64, 64, 512, 8, 64, 256, 128, 32, 512, 128, 128, 3072])
    aq, ak, av, iq, iw, ik, bq, bkv, bkr, cq, ck, cv, gates = [w[:, o[i]:o[i + 1]] for i in range(13)]
    return jnp.concatenate([aq, ak, av, iq, ik, iw, z(56), bq, bkv, bkr, z(96), cq, ck, cv, gates],
                           axis=1).astype(BF16)


def _pack_wq2(w_uq, w_uk):
    hp = lax.Precision.HIGHEST
    w3 = w_uq.reshape(Q_LORA, N_HEADS, NOPE_DIM + ROPE_DIM)
    lat = jnp.einsum('qhd,chd->qhc', w3[:, :, :NOPE_DIM], w_uk, precision=hp)
    half = ROPE_DIM // 2
    pad = jnp.zeros((Q_LORA, N_HEADS, 128 - ROPE_DIM), F32)
    return jnp.concatenate([lat, w3[:, :, NOPE_DIM:NOPE_DIM + half], w3[:, :, NOPE_DIM + half:], pad],
                           axis=2).reshape(Q_LORA, N_HEADS * 256).astype(BF16)


def _pack_wb(w_uv, w_br_b):
    hp = lax.Precision.HIGHEST
    m = jnp.einsum('chd,hdm->hcm', w_uv, w_br_b.reshape(N_HEADS, HEAD_DIM, D_MODEL), precision=hp)
    return m.reshape(N_HEADS * KV_LORA, D_MODEL).astype(BF16)


def _pack_table(t):
    bits = lax.bitcast_convert_type(t.astype(BF16), jnp.uint16).astype(jnp.uint32)
    half = t.shape[1] // 2
    return jnp.bitwise_or(jnp.left_shift(bits[:, :half], 16), bits[:, half:]).reshape(t.shape[0] * TAB_ROWS, 128)


def _rope_tables(pos):
    half = ROPE_DIM // 2
    inv = ROPE_THETA ** (-jnp.arange(half, dtype=F32) / half)
    ang = pos.astype(F32)[:, None] * inv
    cos, sin = jnp.cos(ang), jnp.sin(ang)
    z16 = jnp.zeros_like(cos)
    z96 = jnp.zeros((pos.shape[0], 128 - ROPE_DIM), F32)
    return (jnp.concatenate([cos, cos, z96], axis=1), jnp.concatenate([z16, sin, z96], axis=1),
            jnp.concatenate([-sin, z16, z96], axis=1))


def _t5_bucket(dist):
    n = jnp.maximum(dist, 0)
    exact = N_BUCKETS // 2
    big = exact + (jnp.log(jnp.maximum(n, 1).astype(F32) / exact) / math.log(MAX_DISTANCE / exact)
                   * (N_BUCKETS - exact)).astype(I32)
    return jnp.where(n < exact, n, jnp.minimum(big, N_BUCKETS - 1))


def _prompt_bias_tiles(rel_bias, t):
    i = jnp.arange(t)
    tiles = []
    for delta in range(3):
        d = delta * t + i[:, None] - i[None, :]
        tiles.append(jnp.moveaxis(rel_bias[_t5_bucket(d)], -1, 0))
    return jnp.stack(tiles).astype(F32)


def _sample_bias(rel_bias, past_len, n_tok, n_pages):
    kpos = jnp.arange((n_pages + 1) * 128)
    d = past_len + jnp.arange(n_tok)[:, None] - kpos[None, :]
    b = jnp.moveaxis(rel_bias[_t5_bucket(d)], -1, 0)
    return b.reshape(N_HEADS * n_tok, (n_pages + 1) * 128).astype(F32)


def _pad_rows(x, rows):
    return jnp.pad(x, ((0, 0), (0, rows - x.shape[1]), (0, 0)))


def _keys_on_lanes(x, n_seq, n_tok):
    xt = x.reshape(n_seq, n_tok, -1).transpose(0, 2, 1)
    return jnp.pad(xt, ((0, 0), (0, 0), (0, 128 - n_tok)))


def kernel(x_prompt, x_sample, cache_a_k, cache_a_v, cache_a_idx, cache_b_ckv, cache_b_krope, cache_c_k,
           cache_c_v, page_table, p_prompt, p_sample, rel_bias, g_mix, w_in, g_q, w_uq, g_kv, w_uk, w_uv,
           w_br_a, w_br_b, w_br_c, w_o, g_ffn, w_pq, sub_k1, sub_k2, peer_u, peer_v, g_ple, w_pg, w_pp,
           g_final):
    batch, s_len, _ = x_prompt.shape
    n_seq, n_tok, _ = x_sample.shape
    depth = g_mix.shape[0]
    n_pages, page = page_table.shape[1], cache_a_k.shape[2]
    past_len = n_pages * page
    n_p, n_s = batch * s_len, n_seq * n_tok
    tm = 256
    t_att = 256
    t_sel = 128
    g_pages = math.gcd(16, n_pages)
    tb_peer = 128

    h = jnp.concatenate([x_prompt.reshape(n_p, D_MODEL), x_sample.reshape(n_s, D_MODEL)], axis=0)
    p_all = jnp.concatenate([p_prompt.reshape(depth, n_p, PLE_DIM), p_sample.reshape(depth, n_s, PLE_DIM)], axis=1)
    pos_rows = jnp.concatenate([jnp.arange(s_len), jnp.tile(past_len + jnp.arange(n_tok), tm // n_tok)])
    rope_tabs = _rope_tables(pos_rows)
    bias_tiles = _prompt_bias_tiles(rel_bias, t_att)
    bias_s = _sample_bias(rel_bias, past_len, n_tok, n_pages)
    t_a_k, t_a_v, t_a_idx, t_b_kr = (jnp.swapaxes(c, 2, 3) for c in (cache_a_k, cache_a_v, cache_a_idx, cache_b_krope))
    t_c_k, t_c_v = (jnp.transpose(c, (0, 1, 3, 4, 2)).reshape(c.shape[:2] + (C_KV_HEADS * HEAD_DIM, page))
                    for c in (cache_c_k, cache_c_v))
    row2 = lambda g: g.reshape(1, -1)

    rows_p = [[] for _ in range(7)]
    rows_s = [[] for _ in range(7)]
    y = None
    for i in range(depth):
        (aq, ak, av, iq, ik, iw, qb, ckv, kr, kb, cq, ck, cv, gates) = _project(
            h, row2(g_mix[i]), _pack_w_in(w_in[i]), row2(g_q[i]), _pack_wq2(w_uq[i], w_uk[i]), row2(g_kv[i]),
            rope_tabs, n_p // tm, s_len // tm, tm)
        for lst, r in zip(rows_p, (ak, av, ik, ckv, kr, ck, cv)):
            lst.append(r[:n_p])
        for lst, r in zip(rows_s, (ak, av, ik, ckv, kr, ck, cv)):
            lst.append(r[n_p:])

        mask = _dsa_select(iq, iw, ik, batch, s_len, t_sel)
        oa_p = _dsa_attn(aq, ak, av, mask, bias_tiles, batch, s_len, t_att)
        ob_p = _mla_attn(qb, kb, batch, s_len, t_att)
        oc_p = _sb_attn(cq, ck, cv, batch, s_len, t_att)

        def heads_first(x, width):
            return x[n_p:].reshape(n_seq, n_tok, N_HEADS, width).transpose(0, 2, 1, 3).reshape(
                n_seq, N_HEADS * n_tok, width)

        new_t = lambda x: _keys_on_lanes(x[n_p:], n_seq, n_tok)
        addmask = _samp_select(page_table, heads_first(iq, IDX_DIM), heads_first(iw, 1), new_t(ik),
                               t_a_idx, i, g_pages)
        oa_s = _samp_dsa(page_table, heads_first(aq, HEAD_DIM), addmask, bias_s, new_t(ak), new_t(av),
                         t_a_k, t_a_v, i, g_pages)
        ob_s = _samp_mla(page_table, heads_first(qb, 256), _pad_rows(kb[n_p:].reshape(n_seq, n_tok, -1), 128),
                         cache_b_ckv, t_b_kr, i, g_pages)
        cq_s = cq[n_p:].reshape(n_seq, n_tok, C_KV_HEADS, C_GROUP, HEAD_DIM).transpose(0, 2, 3, 1, 4).reshape(
            n_seq, C_KV_HEADS, C_GROUP * n_tok, HEAD_DIM)
        oc_s = _samp_sb(page_table, cq_s, new_t(ck), new_t(cv), t_c_k, t_c_v, i, g_pages, n_tok)

        def tokens_first(x, width):
            return x.reshape(n_seq, N_HEADS, n_tok, width).transpose(0, 2, 1, 3).reshape(
                n_s, N_HEADS * width).astype(BF16)

        oa = jnp.concatenate([oa_p, tokens_first(oa_s, HEAD_DIM)], axis=0)
        ob = jnp.concatenate([ob_p, tokens_first(ob_s, KV_LORA)], axis=0)
        oc = jnp.concatenate([oc_p, tokens_first(oc_s, HEAD_DIM)], axis=0)

        h1, n2, s_keys = _merge(h, oa, ob, oc, gates, w_br_a[i].astype(BF16), _pack_wb(w_uv[i], w_br_b[i]),
                                w_br_c[i].astype(BF16), w_o[i].astype(BF16), row2(g_ffn[i]),
                                w_pq[i].astype(BF16), sub_k1[i].astype(BF16), sub_k2[i].astype(BF16), tm)
        e_t, g_t = _peer_topk(s_keys, math.gcd(512, n_p + n_s))
        e_tok = e_t.T * TAB_ROWS
        w_t = _peer_u(e_tok, n2.reshape(-1, 8, 128), g_t, _pack_table(peer_u[i]), tb_peer)
        po = _peer_v(e_tok, w_t, _pack_table(peer_v[i]), tb_peer)
        h, y = _ple(h1, po.reshape(-1, D_MODEL), p_all[i], row2(g_ple[i]), w_pg[i].astype(BF16),
                    w_pp[i].astype(BF16), row2(g_final), tm)

    def stack_rows(lst, lead, tail):
        return jnp.stack(lst, axis=0).reshape((depth,) + lead + tail)

    tails = [(HEAD_DIM,), (HEAD_DIM,), (IDX_DIM,), (KV_LORA,), (ROPE_DIM,), (C_KV_HEADS, HEAD_DIM),
             (C_KV_HEADS, HEAD_DIM)]
    out_p = [stack_rows(l, (batch, s_len), t) for l, t in zip(rows_p, tails)]
    out_s = [stack_rows(l, (n_seq, n_tok), t) for l, t in zip(rows_s, tails)]
    return (y[:n_p].reshape(batch, s_len, D_MODEL), y[n_p:].reshape(n_seq, n_tok, D_MODEL), *out_p, *out_s)
```

```python
import functools
import math

import jax
import jax.numpy as jnp
import numpy as np
from jax import lax
from jax.experimental import pallas as pl
from jax.experimental.pallas import tpu as pltpu

F32 = jnp.float32
BF16 = jnp.bfloat16
I32 = jnp.int32

D_MODEL = 1024
HEAD_DIM = 64
EPS = 1e-6
N_HEADS = 8
IDX_DIM = 64
TOPK_MAX = 256
N_BUCKETS = 32
MAX_DISTANCE = 128
Q_LORA = 256
KV_LORA = 128
NOPE_DIM = 64
ROPE_DIM = 32
ROPE_THETA = 10000.0
MLA_SCALE = (NOPE_DIM + ROPE_DIM) ** -0.5
IDX_SCALE = IDX_DIM ** -0.5 * N_HEADS ** -0.5
C_KV_HEADS = 2
C_GROUP = 4
N_KEYS = 128
PEER_TOPK = 16
PEER_PAIRS = N_HEADS * PEER_TOPK
TAB_ROWS = 4
PLE_DIM = 256

NEG = -1e30
INT_MIN = -(2 ** 31)
VMEM_LIMIT = 56 * 1024 * 1024

C_AQ, C_AKV, C_IQ, C_IKW, C_BQ, C_BKV, C_BKR, C_CQ, C_CK, C_CV, C_GATE, C_END = (
    0, 512, 640, 1152, 1280, 1536, 1664, 1792, 2304, 2432, 2560, 5632)

NT = (((1,), (1,)), ((), ()))


def _cparams(sem):
    return pltpu.CompilerParams(dimension_semantics=sem, vmem_limit_bytes=VMEM_LIMIT)


def _rms(x, g):
    return x * lax.rsqrt(jnp.mean(x * x, axis=-1, keepdims=True) + EPS) * g


def _dot(a, b):
    return jnp.dot(a, b, preferred_element_type=F32)


def _dot_nt(a, b):
    return lax.dot_general(a, b, NT, preferred_element_type=F32)


def _project_kernel(x_ref, gmix_ref, w_ref, gq_ref, wq2_ref, gkv_ref, rc_ref, rsa_ref, rsb_ref,
                    aq_ref, ak_ref, av_ref, iq_ref, ik_ref, iw_ref, qb_ref, ckv_ref, kr_ref, kb_ref,
                    cq_ref, ck_ref, cv_ref, gate_ref):
    n = _rms(x_ref[...], gmix_ref[...]).astype(BF16)

    def seg(a, b):
        return _dot(n, w_ref[:, a:b])

    aq_ref[...] = (seg(C_AQ, C_AKV) * HEAD_DIM ** -0.5).astype(BF16)
    t = seg(C_AKV, C_IQ)
    ak_ref[...] = t[:, :HEAD_DIM]
    av_ref[...] = t[:, HEAD_DIM:]
    iq_ref[...] = seg(C_IQ, C_IKW).astype(BF16)
    t = seg(C_IKW, C_BQ)
    ik_ref[...] = t[:, :IDX_DIM]
    iw_ref[...] = t[:, IDX_DIM:IDX_DIM + N_HEADS] * IDX_SCALE

    rc, rsa, rsb = rc_ref[...], rsa_ref[...], rsb_ref[...]

    def rot(r):
        return r * rc + pltpu.roll(r, 16, 1) * rsa + pltpu.roll(r, 112, 1) * rsb

    nq = _rms(seg(C_BQ, C_BKV), gq_ref[...]).astype(BF16)
    for h in range(N_HEADS):
        z = _dot(nq, wq2_ref[:, h * 256:(h + 1) * 256])
        qb_ref[:, h * 256:h * 256 + 128] = (z[:, :128] * MLA_SCALE).astype(BF16)
        qb_ref[:, h * 256 + 128:(h + 1) * 256] = (rot(z[:, 128:]) * MLA_SCALE).astype(BF16)
    c = _rms(seg(C_BKV, C_BKR), gkv_ref[...])
    ckv_ref[...] = c
    kr = rot(seg(C_BKR, C_CQ))
    kr_ref[...] = kr[:, :ROPE_DIM]
    kb_ref[:, :128] = c.astype(BF16)
    kb_ref[:, 128:] = kr.astype(BF16)
    cq_ref[...] = (seg(C_CQ, C_CK) * HEAD_DIM ** -0.5).astype(BF16)
    ck_ref[...] = seg(C_CK, C_CV)
    cv_ref[...] = seg(C_CV, C_GATE)
    gate_ref[...] = jax.nn.sigmoid(seg(C_GATE, C_END)).astype(BF16)


def _project(h, gmix, w_main, gq, wq2, gkv, rope_tabs, n_prompt_blocks, rope_period, tm):
    n_tok = h.shape[0]
    rc, rsa, rsb = rope_tabs

    def rope_map(i):
        return (jnp.where(i < n_prompt_blocks, i % rope_period, rope_period), 0)

    row = lambda w: pl.BlockSpec((tm, w), lambda i: (i, 0))
    full = lambda a: pl.BlockSpec(a.shape, lambda i: (0,) * a.ndim)
    widths = [(512, BF16), (64, F32), (64, F32), (512, BF16), (64, F32), (8, F32), (2048, BF16),
              (128, F32), (32, F32), (256, BF16), (512, BF16), (128, F32), (128, F32), (3072, BF16)]
    return pl.pallas_call(
        _project_kernel,
        grid=(n_tok // tm,),
        in_specs=[row(D_MODEL), full(gmix), full(w_main), full(gq), full(wq2), full(gkv),
                  pl.BlockSpec((tm, 128), rope_map), pl.BlockSpec((tm, 128), rope_map),
                  pl.BlockSpec((tm, 128), rope_map)],
        out_specs=[row(w) for w, _ in widths],
        out_shape=[jax.ShapeDtypeStruct((n_tok, w), d) for w, d in widths],
        compiler_params=_cparams(("parallel",)),
        name="project",
    )(h, gmix, w_main, gq, wq2, gkv, rc, rsa, rsb)


def _sortable(x):
    b = pltpu.bitcast(x, I32)
    k = jnp.where(b >= 0, b, b ^ 0x7FFFFFFF)
    return jnp.where(b == INT_MIN, 0, k)


def _topk_threshold(count, p_ref, k_sel, idx_bits):
    kf = float(k_sel)
    t0 = jnp.where(count(lambda k, i: k >= 0) >= kf, 0, INT_MIN).astype(I32)

    def bis(i, t):
        cand = t + jnp.left_shift(jnp.int32(1), 30 - i)
        return jnp.where(count(lambda k, i_: k >= cand) >= kf, cand, t)

    thr = lax.fori_loop(0, 31, bis, t0)
    need = kf - count(lambda k, i: k > thr)
    excess = jnp.logical_and(count(lambda k, i: k == thr) > need, thr != INT_MIN)
    p_ref[...] = jnp.full(p_ref.shape, 2 ** idx_bits, I32)

    @pl.when(jnp.max(excess.astype(F32)) > 0.0)
    def _():
        def bis_idx(i, p):
            cand = p + jnp.left_shift(jnp.int32(1), idx_bits - 1 - i)
            c = count(lambda k, i_: jnp.logical_and(k == thr, i_ < cand))
            return jnp.where(c < need, cand, p)

        p = lax.fori_loop(0, idx_bits, bis_idx, jnp.zeros(p_ref.shape, I32))
        p_ref[...] = jnp.where(excess, p, 2 ** idx_bits)

    return thr, p_ref[...]


def _selected(key, idx, thr, bound):
    return jnp.logical_or(key > thr, jnp.logical_and(key == thr, idx <= bound))


def _dsa_select_kernel(iq_ref, iw_ref, ik_ref, mask_ref, key_scr, p_scr, *, k_sel, idx_bits):
    tq, s_len = mask_ref.shape
    qi = pl.program_id(1)
    ik = ik_ref[...].astype(BF16)
    iq = iq_ref[...]
    iw = iw_ref[...]
    sc = jnp.zeros((tq, s_len), F32)
    for h in range(N_HEADS):
        s = _dot_nt(iq[:, h * IDX_DIM:(h + 1) * IDX_DIM], ik)
        sc = sc + iw[:, h:h + 1] * jnp.maximum(s, 0.0)
    row = qi * tq + lax.broadcasted_iota(I32, (tq, s_len), 0)
    col = lax.broadcasted_iota(I32, (tq, s_len), 1)
    causal = col <= row
    key = jnp.where(causal, _sortable(sc), INT_MIN)
    n_chunks = s_len // 128
    for c in range(n_chunks):
        key_scr[c] = key[:, c * 128:(c + 1) * 128]
    lane = lax.broadcasted_iota(I32, (tq, 128), 1)
    live = (qi * tq + tq + 127) // 128

    def count(pred):
        acc = lax.fori_loop(0, live, lambda c, a: a + pred(key_scr[c], c * 128 + lane).astype(F32),
                            jnp.zeros((tq, 128), F32))
        return jnp.sum(acc, axis=1, keepdims=True)

    thr, bound = _topk_threshold(count, p_scr, k_sel, idx_bits)
    mask_ref[...] = jnp.where(jnp.logical_and(_selected(key, col, thr, bound), causal), 1.0, 0.0).astype(BF16)


def _dsa_select(iq, iw, ik, batch, s_len, tq):
    nq = s_len // tq
    k_sel = min(TOPK_MAX, s_len // 4)
    idx_bits = max(1, (s_len - 1).bit_length())
    return pl.pallas_call(
        functools.partial(_dsa_select_kernel, k_sel=k_sel, idx_bits=idx_bits),
        grid=(batch, nq),
        in_specs=[pl.BlockSpec((tq, 512), lambda b, q: (b * nq + q, 0)),
                  pl.BlockSpec((tq, N_HEADS), lambda b, q: (b * nq + q, 0)),
                  pl.BlockSpec((s_len, IDX_DIM), lambda b, q: (b, 0))],
        out_specs=pl.BlockSpec((tq, s_len), lambda b, q: (b * nq + q, 0)),
        out_shape=jax.ShapeDtypeStruct((batch * s_len, s_len), BF16),
        scratch_shapes=[pltpu.VMEM((s_len // 128, tq, 128), I32), pltpu.VMEM((tq, 1), I32)],
        compiler_params=_cparams(("parallel", "parallel")),
        name="dsa_select",
    )(iq, iw, ik)


def _softmax_update(s, m_old, l_old, acc_old, pv):
    m_new = jnp.maximum(m_old, jnp.max(s, axis=1, keepdims=True))
    alpha = jnp.exp(m_old - m_new)
    p = jnp.exp(s - pltpu.repeat(m_new, s.shape[1] // 128, axis=1))
    l_new = alpha * l_old + jnp.sum(p, axis=1, keepdims=True)
    return m_new, l_new, alpha[:, :acc_old.shape[1]] * acc_old + pv(p.astype(BF16))


def _heads_update(m_scr, l_scr, acc_scr, logits, pv):
    state = [(m_scr[h], l_scr[h], acc_scr[h]) for h in range(N_HEADS)]
    new = [_softmax_update(logits(h), *state[h], pv) for h in range(N_HEADS)]
    for h in range(N_HEADS):
        m_scr[h], l_scr[h], acc_scr[h] = new[h]


def _causal_pairs(nb, descending=False):
    qs, ks = [], []
    for q in range(nb):
        for k in (range(q, -1, -1) if descending else range(q + 1)):
            qs.append(q)
            ks.append(k)
    return jnp.asarray(qs, I32), jnp.asarray(ks, I32)


def _dsa_attn_kernel(qt_ref, kt_ref, q_ref, k_ref, v_ref, mask_ref, bias_ref, o_ref, qh_scr, m_scr, l_scr,
                     acc_scr):
    p = pl.program_id(1)
    qi, ki = qt_ref[p], kt_ref[p]

    @pl.when(ki == 0)
    def _():
        _init_softmax_state(m_scr, l_scr, acc_scr)
        for h in range(N_HEADS):
            qh_scr[h] = q_ref[:, h * HEAD_DIM:(h + 1) * HEAD_DIM]

    k = k_ref[...].astype(BF16)
    v = v_ref[...].astype(BF16)
    msk = mask_ref[...] > 0
    tix = jnp.minimum(qi - ki, 2)
    _heads_update(m_scr, l_scr, acc_scr,
                  lambda h: jnp.where(msk, _dot_nt(qh_scr[h], k) + bias_ref[tix, h], NEG),
                  lambda pr: _dot(pr, v))

    @pl.when(ki == qi)
    def _():
        for h in range(N_HEADS):
            o_ref[:, h * HEAD_DIM:(h + 1) * HEAD_DIM] = (acc_scr[h] / l_scr[h][:, :HEAD_DIM]).astype(BF16)


def _dsa_attn(aq, ak, av, mask, bias_tiles, batch, s_len, t):
    nb = s_len // t
    qt, kt = _causal_pairs(nb)
    grid_spec = pltpu.PrefetchScalarGridSpec(
        num_scalar_prefetch=2,
        grid=(batch, qt.shape[0]),
        in_specs=[pl.BlockSpec((t, 512), lambda b, p, qt, kt: (b * nb + qt[p], 0)),
                  pl.BlockSpec((t, HEAD_DIM), lambda b, p, qt, kt: (b * nb + kt[p], 0)),
                  pl.BlockSpec((t, HEAD_DIM), lambda b, p, qt, kt: (b * nb + kt[p], 0)),
                  pl.BlockSpec((t, t), lambda b, p, qt, kt: (b * nb + qt[p], kt[p])),
                  pl.BlockSpec(bias_tiles.shape, lambda b, p, qt, kt: (0, 0, 0, 0))],
        out_specs=pl.BlockSpec((t, 512), lambda b, p, qt, kt: (b * nb + qt[p], 0)),
        scratch_shapes=[pltpu.VMEM((N_HEADS, t, HEAD_DIM), BF16), pltpu.VMEM((N_HEADS, t, 128), F32),
                        pltpu.VMEM((N_HEADS, t, 128), F32), pltpu.VMEM((N_HEADS, t, HEAD_DIM), F32)],
    )
    return pl.pallas_call(
        _dsa_attn_kernel,
        grid_spec=grid_spec,
        out_shape=jax.ShapeDtypeStruct((batch * s_len, 512), BF16),
        compiler_params=_cparams(("parallel", "arbitrary")),
        name="dsa_attn",
    )(qt, kt, aq, ak, av, mask, bias_tiles)


def _mla_attn_kernel(qt_ref, kt_ref, q_ref, kb_ref, o_ref, m_scr, l_scr, acc_scr):
    p = pl.program_id(1)
    qi, ki = qt_ref[p], kt_ref[p]
    t = kb_ref.shape[0]

    @pl.when(ki == 0)
    def _():
        _init_softmax_state(m_scr, l_scr, acc_scr)

    kb = kb_ref[...]
    v = kb[:, :KV_LORA]
    logits = lambda h: _dot_nt(q_ref[:, h * 256:(h + 1) * 256], kb)

    @pl.when(ki != qi)
    def _():
        _heads_update(m_scr, l_scr, acc_scr, logits, lambda pr: _dot(pr, v))

    @pl.when(ki == qi)
    def _():
        msk = lax.broadcasted_iota(I32, (t, t), 1) <= lax.broadcasted_iota(I32, (t, t), 0)
        _heads_update(m_scr, l_scr, acc_scr, lambda h: jnp.where(msk, logits(h), NEG), lambda pr: _dot(pr, v))
        for h in range(N_HEADS):
            o_ref[:, h * KV_LORA:(h + 1) * KV_LORA] = (acc_scr[h] / l_scr[h]).astype(BF16)


def _mla_attn(qb, kb, batch, s_len, t):
    nb = s_len // t
    qt, kt = _causal_pairs(nb)
    grid_spec = pltpu.PrefetchScalarGridSpec(
        num_scalar_prefetch=2,
        grid=(batch, qt.shape[0]),
        in_specs=[pl.BlockSpec((t, 2048), lambda b, p, qt, kt: (b * nb + qt[p], 0)),
                  pl.BlockSpec((t, 256), lambda b, p, qt, kt: (b * nb + kt[p], 0))],
        out_specs=pl.BlockSpec((t, 1024), lambda b, p, qt, kt: (b * nb + qt[p], 0)),
        scratch_shapes=[pltpu.VMEM((N_HEADS, t, 128), F32), pltpu.VMEM((N_HEADS, t, 128), F32),
                        pltpu.VMEM((N_HEADS, t, KV_LORA), F32)],
    )
    return pl.pallas_call(
        _mla_attn_kernel,
        grid_spec=grid_spec,
        out_shape=jax.ShapeDtypeStruct((batch * s_len, 1024), BF16),
        compiler_params=_cparams(("parallel", "arbitrary")),
        name="mla_attn",
    )(qt, kt, qb, kb)


def _sb_block(z, strict, tri, carry, pv):
    lk = -(jnp.maximum(z, 0.0) + jnp.log(1.0 + jnp.exp(-jnp.abs(z))))
    if strict is not None:
        lk = jnp.where(strict, lk, 0.0)
    hi = lk.astype(BF16)
    lo = (lk - hi.astype(F32)).astype(BF16)
    cs = _dot(hi, tri) + _dot(lo, tri)
    a = jnp.exp(z + cs + pltpu.repeat(carry, z.shape[1] // 128, axis=1))
    if strict is not None:
        a = jnp.where(strict, a, 0.0)
    return pv(a.astype(BF16)), carry + jnp.sum(lk, axis=1, keepdims=True)


def _tri(n):
    return jnp.where(lax.broadcasted_iota(I32, (n, n), 0) >= lax.broadcasted_iota(I32, (n, n), 1),
                     1.0, 0.0).astype(BF16)


def _sb_attn_kernel(qt_ref, kt_ref, q_ref, k_ref, v_ref, o_ref, qh_scr, carry_scr, acc_scr):
    p = pl.program_id(1)
    qi, ki = qt_ref[p], kt_ref[p]
    t = k_ref.shape[0]

    def update(strict):
        k = k_ref[...].astype(BF16)
        v = v_ref[...].astype(BF16)
        tri = _tri(t)
        kn = [k[:, n * HEAD_DIM:(n + 1) * HEAD_DIM] for n in range(C_KV_HEADS)]
        vn = [v[:, n * HEAD_DIM:(n + 1) * HEAD_DIM] for n in range(C_KV_HEADS)]
        state = [(carry_scr[h], acc_scr[h]) for h in range(N_HEADS)]
        new = []
        for h in range(N_HEADS):
            n = h // C_GROUP
            o, c = _sb_block(_dot_nt(qh_scr[h], kn[n]), strict, tri, state[h][0], lambda a, n=n: _dot(a, vn[n]))
            new.append((c, state[h][1] + o))
        for h in range(N_HEADS):
            carry_scr[h], acc_scr[h] = new[h]

    @pl.when(ki == qi)
    def _():
        carry_scr[...] = jnp.zeros(carry_scr.shape, F32)
        acc_scr[...] = jnp.zeros(acc_scr.shape, F32)
        for h in range(N_HEADS):
            qh_scr[h] = q_ref[:, h * HEAD_DIM:(h + 1) * HEAD_DIM]
        update(lax.broadcasted_iota(I32, (t, t), 1) < lax.broadcasted_iota(I32, (t, t), 0))

    @pl.when(ki != qi)
    def _():
        update(None)

    @pl.when(ki == 0)
    def _():
        for h in range(N_HEADS):
            o_ref[:, h * HEAD_DIM:(h + 1) * HEAD_DIM] = acc_scr[h].astype(BF16)


def _sb_attn(cq, ck, cv, batch, s_len, t):
    nb = s_len // t
    qt, kt = _causal_pairs(nb, descending=True)
    kmap = lambda b, p, qt, kt: (b * nb + kt[p], 0)
    grid_spec = pltpu.PrefetchScalarGridSpec(
        num_scalar_prefetch=2,
        grid=(batch, qt.shape[0]),
        in_specs=[pl.BlockSpec((t, 512), lambda b, p, qt, kt: (b * nb + qt[p], 0)),
                  pl.BlockSpec((t, 128), kmap), pl.BlockSpec((t, 128), kmap)],
        out_specs=pl.BlockSpec((t, 512), lambda b, p, qt, kt: (b * nb + qt[p], 0)),
        scratch_shapes=[pltpu.VMEM((N_HEADS, t, HEAD_DIM), BF16), pltpu.VMEM((N_HEADS, t, 128), F32),
                        pltpu.VMEM((N_HEADS, t, HEAD_DIM), F32)],
    )
    return pl.pallas_call(
        _sb_attn_kernel,
        grid_spec=grid_spec,
        out_shape=jax.ShapeDtypeStruct((batch * s_len, 512), BF16),
        compiler_params=_cparams(("parallel", "arbitrary")),
        name="sb_attn",
    )(qt, kt, cq, ck, cv)


def _page_specs(cache, layer, g_pages, n_pages, page_of):
    page, width = cache.shape[2], cache.shape[3]

    def spec(g):
        return pl.BlockSpec((None, None, page, width),
                            lambda b, j, pt: (layer, pt[b * n_pages + page_of(j, g)], 0, 0))

    return [spec(g) for g in range(g_pages)]


def _samp_select_kernel(pt_ref, iq_ref, iw_ref, iknew_ref, *rest, g_pages, n_pages, k_sel, idx_bits):
    pages = rest[:g_pages]
    out_ref, sc_scr, key_scr, p_scr = rest[g_pages:]
    j = pl.program_id(1)
    ns = pl.num_programs(1)
    q = iq_ref[...]
    w = iw_ref[...]
    n_tok = out_ref.shape[0]

    def score(kblk_t):
        s = jnp.maximum(_dot(q, kblk_t.astype(BF16)), 0.0) * w
        tot = s[0:n_tok]
        for h in range(1, N_HEADS):
            tot = tot + s[h * n_tok:(h + 1) * n_tok]
        return tot

    for g in range(g_pages):
        sc_scr[j * g_pages + g] = score(pages[g][...])

    @pl.when(j == ns - 1)
    def _():
        sc_scr[n_pages] = score(iknew_ref[...])
        shape = sc_scr.shape
        page_i = lax.broadcasted_iota(I32, shape, 0)
        row = lax.broadcasted_iota(I32, shape, 1)
        lane = lax.broadcasted_iota(I32, shape, 2)
        causal = jnp.logical_or(page_i < n_pages, lane <= row)
        key_scr[...] = jnp.where(causal, _sortable(sc_scr[...]), INT_MIN)
        idx = page_i * 128 + lane

        def count(pred):
            m = pred(key_scr[...], idx).astype(F32)
            return jnp.sum(jnp.sum(m, axis=0), axis=1, keepdims=True)[None]

        thr, bound = _topk_threshold(count, p_scr, k_sel, idx_bits)
        sel = _selected(key_scr[...], idx, thr, bound)
        key_scr[...] = jnp.where(jnp.logical_and(sel, causal), 1, 0)
        for p in range(n_pages + 1):
            out_ref[:, p * 128:(p + 1) * 128] = jnp.where(key_scr[p] > 0, 0.0, NEG)


def _samp_select(page_table, iq_s, iw_s, ik_new, cache_idx, layer, g_pages):
    n_seq, n_tok = ik_new.shape[0], iq_s.shape[1] // N_HEADS
    n_pages = page_table.shape[1]
    ns = n_pages // g_pages
    n_keys = n_pages * 128 + n_tok
    k_sel = min(TOPK_MAX, n_keys // 4)
    idx_bits = ((n_pages + 1) * 128 - 1).bit_length()
    grid_spec = pltpu.PrefetchScalarGridSpec(
        num_scalar_prefetch=1,
        grid=(n_seq, ns),
        in_specs=[pl.BlockSpec((None, N_HEADS * n_tok, IDX_DIM), lambda b, j, pt: (b, 0, 0)),
                  pl.BlockSpec((None, N_HEADS * n_tok, 1), lambda b, j, pt: (b, 0, 0)),
                  pl.BlockSpec((None, IDX_DIM, 128), lambda b, j, pt: (b, 0, 0))]
        + _page_specs(cache_idx, layer, g_pages, n_pages, lambda j, g: j * g_pages + g),
        out_specs=pl.BlockSpec((None, n_tok, (n_pages + 1) * 128), lambda b, j, pt: (b, 0, 0)),
        scratch_shapes=[pltpu.VMEM((n_pages + 1, n_tok, 128), F32),
                        pltpu.VMEM((n_pages + 1, n_tok, 128), I32),
                        pltpu.VMEM((1, n_tok, 1), I32)],
    )
    return pl.pallas_call(
        functools.partial(_samp_select_kernel, g_pages=g_pages, n_pages=n_pages, k_sel=k_sel,
                          idx_bits=idx_bits),
        grid_spec=grid_spec,
        out_shape=jax.ShapeDtypeStruct((n_seq, n_tok, (n_pages + 1) * 128), F32),
        compiler_params=_cparams(("parallel", "arbitrary")),
        name="samp_select",
    )(page_table.reshape(-1), iq_s, iw_s, ik_new, *([cache_idx] * g_pages))


def _state_update(s, m_scr, l_scr, acc_scr, pv):
    m_scr[...], l_scr[...], acc_scr[...] = _softmax_update(s, m_scr[...], l_scr[...], acc_scr[...], pv)


def _init_softmax_state(m_scr, l_scr, acc_scr):
    m_scr[...] = jnp.full(m_scr.shape, NEG, F32)
    l_scr[...] = jnp.zeros(l_scr.shape, F32)
    acc_scr[...] = jnp.zeros(acc_scr.shape, F32)


def _lanes(x, g):
    return x[:, g * 128:(g + 1) * 128]


def _samp_dsa_kernel(pt_ref, q_ref, am_ref, amnew_ref, bias_ref, biasnew_ref, knew_ref, vnew_ref, *rest,
                     g_pages):
    kp = rest[:g_pages]
    vp = rest[g_pages:2 * g_pages]
    o_ref, m_scr, l_scr, acc_scr = rest[2 * g_pages:]
    j = pl.program_id(1)
    ns = pl.num_programs(1)
    q = q_ref[...]

    @pl.when(j == 0)
    def _():
        _init_softmax_state(m_scr, l_scr, acc_scr)

    s = jnp.concatenate([_dot(q, kp[g][...].astype(BF16)) for g in range(g_pages)], axis=1)
    s = s + bias_ref[...] + jnp.concatenate([am_ref[...]] * N_HEADS, axis=0)

    def pv(p):
        out = _dot_nt(_lanes(p, 0), vp[0][...].astype(BF16))
        for g in range(1, g_pages):
            out = out + _dot_nt(_lanes(p, g), vp[g][...].astype(BF16))
        return out

    _state_update(s, m_scr, l_scr, acc_scr, pv)

    @pl.when(j == ns - 1)
    def _():
        s_new = (_dot(q, knew_ref[...].astype(BF16)) + biasnew_ref[...]
                 + jnp.concatenate([amnew_ref[...]] * N_HEADS, axis=0))
        _state_update(s_new, m_scr, l_scr, acc_scr, lambda p: _dot_nt(p, vnew_ref[...].astype(BF16)))
        o_ref[...] = acc_scr[...] / l_scr[...][:, :HEAD_DIM]


def _samp_dsa(page_table, q_s, addmask, bias_s, k_new, v_new, cache_k, cache_v, layer, g_pages):
    n_seq, rows = q_s.shape[0], q_s.shape[1]
    n_pages = page_table.shape[1]
    n_tok = rows // N_HEADS
    wide = g_pages * 128
    page_of = lambda j, g: j * g_pages + g
    grid_spec = pltpu.PrefetchScalarGridSpec(
        num_scalar_prefetch=1,
        grid=(n_seq, n_pages // g_pages),
        in_specs=[pl.BlockSpec((None, rows, HEAD_DIM), lambda b, j, pt: (b, 0, 0)),
                  pl.BlockSpec((None, n_tok, wide), lambda b, j, pt: (b, 0, j)),
                  pl.BlockSpec((None, n_tok, 128), lambda b, j, pt: (b, 0, n_pages)),
                  pl.BlockSpec((rows, wide), lambda b, j, pt: (0, j)),
                  pl.BlockSpec((rows, 128), lambda b, j, pt: (0, n_pages)),
                  pl.BlockSpec((None, HEAD_DIM, 128), lambda b, j, pt: (b, 0, 0)),
                  pl.BlockSpec((None, HEAD_DIM, 128), lambda b, j, pt: (b, 0, 0))]
        + _page_specs(cache_k, layer, g_pages, n_pages, page_of)
        + _page_specs(cache_v, layer, g_pages, n_pages, page_of),
        out_specs=pl.BlockSpec((None, rows, HEAD_DIM), lambda b, j, pt: (b, 0, 0)),
        scratch_shapes=[pltpu.VMEM((rows, 128), F32), pltpu.VMEM((rows, 128), F32),
                        pltpu.VMEM((rows, HEAD_DIM), F32)],
    )
    return pl.pallas_call(
        functools.partial(_samp_dsa_kernel, g_pages=g_pages),
        grid_spec=grid_spec,
        out_shape=jax.ShapeDtypeStruct((n_seq, rows, HEAD_DIM), F32),
        compiler_params=_cparams(("parallel", "arbitrary")),
        name="samp_dsa",
    )(page_table.reshape(-1), q_s, addmask, addmask, bias_s, bias_s, k_new, v_new,
      *([cache_k] * g_pages), *([cache_v] * g_pages))


def _samp_mla_kernel(pt_ref, q_ref, kbnew_ref, *rest, g_pages, n_tok):
    cp = rest[:g_pages]
    rp = rest[g_pages:2 * g_pages]
    o_ref, m_scr, l_scr, acc_scr = rest[2 * g_pages:]
    j = pl.program_id(1)
    ns = pl.num_programs(1)
    q = q_ref[...]

    @pl.when(j == 0)
    def _():
        _init_softmax_state(m_scr, l_scr, acc_scr)

    cs = [cp[g][...].astype(BF16) for g in range(g_pages)]
    q_lat, q_rope = q[:, :KV_LORA], q[:, KV_LORA:KV_LORA + ROPE_DIM]
    s = jnp.concatenate([_dot_nt(q_lat, cs[g]) + _dot(q_rope, rp[g][...].astype(BF16))
                         for g in range(g_pages)], axis=1)

    def pv(p):
        out = _dot(_lanes(p, 0), cs[0])
        for g in range(1, g_pages):
            out = out + _dot(_lanes(p, g), cs[g])
        return out

    _state_update(s, m_scr, l_scr, acc_scr, pv)

    @pl.when(j == ns - 1)
    def _():
        kb = kbnew_ref[...]
        s_new = _dot_nt(q, kb)
        row = lax.broadcasted_iota(I32, s_new.shape, 0) % n_tok
        col = lax.broadcasted_iota(I32, s_new.shape, 1)
        _state_update(jnp.where(col <= row, s_new, NEG), m_scr, l_scr, acc_scr,
                      lambda p: _dot(p, kb[:, :KV_LORA]))
        o_ref[...] = acc_scr[...] / l_scr[...]


def _samp_mla(page_table, q_s, kb_new, cache_ckv, cache_kr, layer, g_pages):
    n_seq, rows = q_s.shape[0], q_s.shape[1]
    n_pages = page_table.shape[1]
    page_of = lambda j, g: j * g_pages + g
    grid_spec = pltpu.PrefetchScalarGridSpec(
        num_scalar_prefetch=1,
        grid=(n_seq, n_pages // g_pages),
        in_specs=[pl.BlockSpec((None, rows, 256), lambda b, j, pt: (b, 0, 0)),
                  pl.BlockSpec((None, 128, 256), lambda b, j, pt: (b, 0, 0))]
        + _page_specs(cache_ckv, layer, g_pages, n_pages, page_of)
        + _page_specs(cache_kr, layer, g_pages, n_pages, page_of),
        out_specs=pl.BlockSpec((None, rows, KV_LORA), lambda b, j, pt: (b, 0, 0)),
        scratch_shapes=[pltpu.VMEM((rows, 128), F32), pltpu.VMEM((rows, 128), F32),
                        pltpu.VMEM((rows, KV_LORA), F32)],
    )
    return pl.pallas_call(
        functools.partial(_samp_mla_kernel, g_pages=g_pages, n_tok=rows // N_HEADS),
        grid_spec=grid_spec,
        out_shape=jax.ShapeDtypeStruct((n_seq, rows, KV_LORA), F32),
        compiler_params=_cparams(("parallel", "arbitrary")),
        name="samp_mla",
    )(page_table.reshape(-1), q_s, kb_new, *([cache_ckv] * g_pages), *([cache_kr] * g_pages))


def _samp_sb_kernel(pt_ref, q_ref, knew_ref, vnew_ref, *rest, g_pages, n_tok):
    kp = rest[:g_pages]
    vp = rest[g_pages:2 * g_pages]
    o_ref, carry_scr, acc_scr = rest[2 * g_pages:]
    j = pl.program_id(1)
    ns = pl.num_programs(1)
    rows = q_ref.shape[1]
    tri = _tri(128)
    head = lambda x, n: x[n * HEAD_DIM:(n + 1) * HEAD_DIM, :].astype(BF16)

    @pl.when(j == 0)
    def _():
        carry_scr[...] = jnp.zeros(carry_scr.shape, F32)
        acc_scr[...] = jnp.zeros(acc_scr.shape, F32)
        row = lax.broadcasted_iota(I32, (rows, 128), 0) % n_tok
        col = lax.broadcasted_iota(I32, (rows, 128), 1)
        for n in range(C_KV_HEADS):
            o, c = _sb_block(_dot(q_ref[n], head(knew_ref[...], n)), col < row, tri, carry_scr[n],
                             lambda a, n=n: _dot_nt(a, head(vnew_ref[...], n)))
            acc_scr[n] = o
            carry_scr[n] = c

    order = list(reversed(range(g_pages)))
    for n in range(C_KV_HEADS):
        z = [_dot(q_ref[n], head(kp[g][...], n)) for g in order]
        lk = [-(jnp.maximum(x, 0.0) + jnp.log(1.0 + jnp.exp(-jnp.abs(x)))) for x in z]
        lk_all = jnp.concatenate(lk, axis=0)
        hi = lk_all.astype(BF16)
        lo = (lk_all - hi.astype(F32)).astype(BF16)
        cs = _dot(hi, tri) + _dot(lo, tri)
        carry = carry_scr[n]
        out = acc_scr[n]
        for i, g in enumerate(order):
            a = jnp.exp(z[i] + cs[i * rows:(i + 1) * rows] + carry)
            out = out + _dot_nt(a.astype(BF16), head(vp[g][...], n))
            carry = carry + jnp.sum(lk[i], axis=1, keepdims=True)
        acc_scr[n] = out
        carry_scr[n] = carry

    @pl.when(j == ns - 1)
    def _():
        o_ref[...] = acc_scr[...]


def _samp_sb(page_table, q_s, k_new, v_new, cache_k, cache_v, layer, g_pages, n_tok):
    n_seq, rows = q_s.shape[0], q_s.shape[2]
    n_pages = page_table.shape[1]
    ns = n_pages // g_pages
    page_of = lambda j, g: (ns - 1 - j) * g_pages + g
    grid_spec = pltpu.PrefetchScalarGridSpec(
        num_scalar_prefetch=1,
        grid=(n_seq, ns),
        in_specs=[pl.BlockSpec((None, C_KV_HEADS, rows, HEAD_DIM), lambda b, j, pt: (b, 0, 0, 0)),
                  pl.BlockSpec((None, 128, 128), lambda b, j, pt: (b, 0, 0)),
                  pl.BlockSpec((None, 128, 128), lambda b, j, pt: (b, 0, 0))]
        + _page_specs(cache_k, layer, g_pages, n_pages, page_of)
        + _page_specs(cache_v, layer, g_pages, n_pages, page_of),
        out_specs=pl.BlockSpec((None, C_KV_HEADS, rows, HEAD_DIM), lambda b, j, pt: (b, 0, 0, 0)),
        scratch_shapes=[pltpu.VMEM((C_KV_HEADS, rows, 128), F32),
                        pltpu.VMEM((C_KV_HEADS, rows, HEAD_DIM), F32)],
    )
    return pl.pallas_call(
        functools.partial(_samp_sb_kernel, g_pages=g_pages, n_tok=n_tok),
        grid_spec=grid_spec,
        out_shape=jax.ShapeDtypeStruct((n_seq, C_KV_HEADS, rows, HEAD_DIM), F32),
        compiler_params=_cparams(("parallel", "arbitrary")),
        name="samp_sb",
    )(page_table.reshape(-1), q_s, k_new, v_new, *([cache_k] * g_pages), *([cache_v] * g_pages))


def _merge_kernel(h_ref, oa_ref, ob_ref, oc_ref, gate_ref, wa_ref, wb_ref, wc_ref, wo_ref, gffn_ref,
                  wpq_ref, sk1_ref, sk2_ref, h1_ref, n2_ref, s_ref):
    g = gate_ref[...].astype(F32)
    m = (g[:, :D_MODEL] * _dot(oa_ref[...], wa_ref[...])
         + g[:, D_MODEL:2 * D_MODEL] * _dot(ob_ref[...], wb_ref[...])
         + g[:, 2 * D_MODEL:] * _dot(oc_ref[...], wc_ref[...]))
    h1 = h_ref[...] + _dot(m.astype(BF16), wo_ref[...])
    h1_ref[...] = h1
    n2 = _rms(h1, gffn_ref[...])
    n2_ref[...] = n2
    q = _dot(n2.astype(BF16), wpq_ref[...])
    for h in range(N_HEADS):
        for j, sk in enumerate((sk1_ref, sk2_ref)):
            qh = q[:, h * 256 + j * 128:h * 256 + (j + 1) * 128].astype(BF16)
            s_ref[j, h] = _dot_nt(sk[...], qh)


def _merge(h, oa, ob, oc, gates, wa, wb, wc, wo, gffn, wpq, sk1, sk2, tm):
    n_tok = h.shape[0]
    row = lambda w: pl.BlockSpec((tm, w), lambda i: (i, 0))
    full = lambda a: pl.BlockSpec(a.shape, lambda i: (0,) * a.ndim)
    return pl.pallas_call(
        _merge_kernel,
        grid=(n_tok // tm,),
        in_specs=[row(D_MODEL), row(512), row(1024), row(512), row(3072), full(wa), full(wb), full(wc),
                  full(wo), full(gffn), full(wpq), full(sk1), full(sk2)],
        out_specs=[row(D_MODEL), row(D_MODEL),
                   pl.BlockSpec((2, N_HEADS, N_KEYS, tm), lambda i: (0, 0, 0, i))],
        out_shape=[jax.ShapeDtypeStruct((n_tok, D_MODEL), F32), jax.ShapeDtypeStruct((n_tok, D_MODEL), F32),
                   jax.ShapeDtypeStruct((2, N_HEADS, N_KEYS, n_tok), F32)],
        compiler_params=_cparams(("parallel",)),
        name="merge",
    )(h, oa, ob, oc, gates, wa, wb, wc, wo, gffn, wpq, sk1, sk2)


def _extract16(s, ids):
    vals, idxs = [], []
    for _ in range(PEER_TOPK):
        m = jnp.max(s, axis=0, keepdims=True)
        idx = jnp.min(jnp.where(s == m, ids, 2 ** 30), axis=0, keepdims=True)
        vals.append(m)
        idxs.append(idx)
        s = jnp.where(ids == idx, -jnp.inf, s)
    return vals, idxs


def _peer_topk_kernel(s_ref, e_ref, g_ref):
    cols = s_ref.shape[-1]
    kio = lax.broadcasted_iota(I32, (N_KEYS, 128), 0)
    for c in range(cols // 128):
        sl = slice(c * 128, (c + 1) * 128)
        v1, i1 = _extract16(s_ref[0, 0, :, sl], kio)
        v2, i2 = _extract16(s_ref[1, 0, :, sl], kio)
        v1a, i1a = jnp.concatenate(v1, axis=0), jnp.concatenate(i1, axis=0)
        v2a, i2a = jnp.concatenate(v2, axis=0), jnp.concatenate(i2, axis=0)
        cv = [v1[0] + v2a] + [v1[a] + v2a[:8] for a in range(1, 8)] + [v1a[8:] + v2[0]]
        ci = [i1[0] * N_KEYS + i2a] + [i1[a] * N_KEYS + i2a[:8] for a in range(1, 8)] + [i1a[8:] * N_KEYS + i2[0]]
        sc, e = _extract16(jnp.concatenate(cv, axis=0), jnp.concatenate(ci, axis=0))
        sc = jnp.concatenate(sc, axis=0)
        p = jnp.exp(sc - sc[0:1])
        e_ref[:, sl] = jnp.concatenate(e, axis=0)
        g_ref[:, sl] = p / jnp.sum(p, axis=0, keepdims=True)


def _peer_topk(s, tc):
    n_tok = s.shape[-1]
    return pl.pallas_call(
        _peer_topk_kernel,
        grid=(N_HEADS, n_tok // tc),
        in_specs=[pl.BlockSpec((2, 1, N_KEYS, tc), lambda h, i: (0, h, 0, i))],
        out_specs=[pl.BlockSpec((PEER_TOPK, tc), lambda h, i: (h, i)),
                   pl.BlockSpec((PEER_TOPK, tc), lambda h, i: (h, i))],
        out_shape=[jax.ShapeDtypeStruct((PEER_PAIRS, n_tok), I32),
                   jax.ShapeDtypeStruct((PEER_PAIRS, n_tok), F32)],
        compiler_params=_cparams(("parallel", "parallel")),
        name="peer_topk",
    )(s)


def _unpack(w):
    hi = pltpu.bitcast(jnp.bitwise_and(w, jnp.uint32(0xFFFF0000)), F32)
    lo = pltpu.bitcast(jnp.left_shift(w, jnp.uint32(16)), F32)
    return hi, lo


def _peer_u_kernel(e_ref, x_ref, g_ref, tab_ref, w_ref, p0_scr, p1_scr):
    tb = x_ref.shape[0]
    lane = lax.broadcasted_iota(I32, (PEER_PAIRS, tb), 1)

    def gather(t, p_scr):
        x = x_ref[t]
        xh, xl = x[0:4], x[4:8]
        for j in range(PEER_PAIRS):
            hi, lo = _unpack(tab_ref[pl.ds(pl.multiple_of(e_ref[t, j], TAB_ROWS), TAB_ROWS), :])
            p_scr[4 * j:4 * j + 4, :] = hi * xh + lo * xl

    def reduce(p_scr, t, acts):
        r = (p_scr[pl.ds(0, PEER_PAIRS, stride=4), :] + p_scr[pl.ds(1, PEER_PAIRS, stride=4), :]
             + p_scr[pl.ds(2, PEER_PAIRS, stride=4), :] + p_scr[pl.ds(3, PEER_PAIRS, stride=4), :])
        return jnp.where(lane == t, jnp.sum(r, axis=1, keepdims=True), acts)

    def two_tokens(i, acts):
        gather(2 * i, p0_scr)
        acts = reduce(p1_scr, 2 * i - 1, acts)
        gather(2 * i + 1, p1_scr)
        return reduce(p0_scr, 2 * i, acts)

    p1_scr[...] = jnp.zeros(p1_scr.shape, F32)
    a = lax.fori_loop(0, tb // 2, two_tokens, jnp.zeros((PEER_PAIRS, tb), F32))
    a = reduce(p1_scr, tb - 1, a)
    w_ref[...] = g_ref[...] * jax.nn.gelu(a)


def _peer_u(e_tok, x3, g_t, tab, tb):
    n_tok = x3.shape[0]
    return pl.pallas_call(
        _peer_u_kernel,
        grid=(n_tok // tb,),
        in_specs=[pl.BlockSpec((tb, PEER_PAIRS), lambda i: (i, 0), memory_space=pltpu.SMEM),
                  pl.BlockSpec((tb, 8, 128), lambda i: (i, 0, 0)),
                  pl.BlockSpec((PEER_PAIRS, tb), lambda i: (0, i)),
                  pl.BlockSpec(memory_space=pltpu.VMEM)],
        out_specs=pl.BlockSpec((PEER_PAIRS, tb), lambda i: (0, i)),
        out_shape=jax.ShapeDtypeStruct((PEER_PAIRS, n_tok), F32),
        scratch_shapes=[pltpu.VMEM((4 * PEER_PAIRS, 128), F32), pltpu.VMEM((4 * PEER_PAIRS, 128), F32)],
        compiler_params=_cparams(("arbitrary",)),
        name="peer_u",
    )(e_tok, x3, g_t, tab)


def _peer_v_kernel(e_ref, w_ref, tab_ref, o_ref, w0_scr, w1_scr):
    tb = o_ref.shape[0]
    lane = lax.broadcasted_iota(I32, (PEER_PAIRS, tb), 1)

    def stage_weights(t, wrep_scr):
        col = jnp.sum(jnp.where(lane == t, w_ref[...], 0.0), axis=1, keepdims=True)
        wrep_scr[...] = jnp.broadcast_to(col, (PEER_PAIRS, 128))

    def gather(t, wrep_scr):
        acc_h = [jnp.zeros((4, 128), F32) for _ in range(4)]
        acc_l = [jnp.zeros((4, 128), F32) for _ in range(4)]
        for j in range(PEER_PAIRS):
            hi, lo = _unpack(tab_ref[pl.ds(pl.multiple_of(e_ref[t, j], TAB_ROWS), TAB_ROWS), :])
            wgt = jnp.broadcast_to(wrep_scr[j:j + 1, :], (4, 128))
            acc_h[j % 4] = acc_h[j % 4] + wgt * hi
            acc_l[j % 4] = acc_l[j % 4] + wgt * lo
        o_ref[t] = jnp.concatenate([(acc_h[0] + acc_h[1]) + (acc_h[2] + acc_h[3]),
                                    (acc_l[0] + acc_l[1]) + (acc_l[2] + acc_l[3])], axis=0)

    def two_tokens(i, carry):
        stage_weights(2 * i + 1, w1_scr)
        gather(2 * i, w0_scr)
        stage_weights(2 * i + 2, w0_scr)
        gather(2 * i + 1, w1_scr)
        return carry

    stage_weights(0, w0_scr)
    lax.fori_loop(0, tb // 2, two_tokens, 0)


def _peer_v(e_tok, w_t, tab, tb):
    n_tok = e_tok.shape[0]
    return pl.pallas_call(
        _peer_v_kernel,
        grid=(n_tok // tb,),
        in_specs=[pl.BlockSpec((tb, PEER_PAIRS), lambda i: (i, 0), memory_space=pltpu.SMEM),
                  pl.BlockSpec((PEER_PAIRS, tb), lambda i: (0, i)),
                  pl.BlockSpec(memory_space=pltpu.VMEM)],
        out_specs=pl.BlockSpec((tb, 8, 128), lambda i: (i, 0, 0)),
        out_shape=jax.ShapeDtypeStruct((n_tok, 8, 128), F32),
        scratch_shapes=[pltpu.VMEM((PEER_PAIRS, 128), F32), pltpu.VMEM((PEER_PAIRS, 128), F32)],
        compiler_params=_cparams(("arbitrary",)),
        name="peer_v",
    )(e_tok, w_t, tab)


def _ple_kernel(h1_ref, po_ref, p_ref, gple_ref, wpg_ref, wpp_ref, gfin_ref, h_ref, y_ref):
    h2 = h1_ref[...] + po_ref[...]
    n = _rms(h2, gple_ref[...]).astype(BF16)
    h3 = h2 + jax.nn.sigmoid(_dot(n, wpg_ref[...])) * _dot(p_ref[...].astype(BF16), wpp_ref[...])
    h_ref[...] = h3
    y_ref[...] = _rms(h3, gfin_ref[...])


def _ple(h1, po, p, gple, wpg, wpp, gfin, tm):
    n_tok = h1.shape[0]
    row = lambda w: pl.BlockSpec((tm, w), lambda i: (i, 0))
    full = lambda a: pl.BlockSpec(a.shape, lambda i: (0,) * a.ndim)
    return pl.pallas_call(
        _ple_kernel,
        grid=(n_tok // tm,),
        in_specs=[row(D_MODEL), row(D_MODEL), row(PLE_DIM), full(gple), full(wpg), full(wpp), full(gfin)],
        out_specs=[row(D_MODEL), row(D_MODEL)],
        out_shape=[jax.ShapeDtypeStruct((n_tok, D_MODEL), F32)] * 2,
        compiler_params=_cparams(("parallel",)),
        name="ple",
    )(h1, po, p, gple, wpg, wpp, gfin)


def _pack_w_in(w):
    z = lambda n: jnp.zeros((w.shape[0], n), w.dtype)
    o = np.cumsum([0, 512, 64, 64, 512, 8, 64, 256, 128, 32, 512, 128, 128, 3072])
    aq, ak, av, iq, iw, ik, bq, bkv, bkr, cq, ck, cv, gates = [w[:, o[i]:o[i + 1]] for i in range(13)]
    return jnp.concatenate([aq, ak, av, iq, ik, iw, z(56), bq, bkv, bkr, z(96), cq, ck, cv, gates],
                           axis=1).astype(BF16)


def _pack_wq2(w_uq, w_uk):
    hp = lax.Precision.HIGHEST
    w3 = w_uq.reshape(Q_LORA, N_HEADS, NOPE_DIM + ROPE_DIM)
    lat = jnp.einsum('qhd,chd->qhc', w3[:, :, :NOPE_DIM], w_uk, precision=hp)
    half = ROPE_DIM // 2
    pad = jnp.zeros((Q_LORA, N_HEADS, 128 - ROPE_DIM), F32)
    return jnp.concatenate([lat, w3[:, :, NOPE_DIM:NOPE_DIM + half], w3[:, :, NOPE_DIM + half:], pad],
                           axis=2).reshape(Q_LORA, N_HEADS * 256).astype(BF16)


def _pack_wb(w_uv, w_br_b):
    hp = lax.Precision.HIGHEST
    m = jnp.einsum('chd,hdm->hcm', w_uv, w_br_b.reshape(N_HEADS, HEAD_DIM, D_MODEL), precision=hp)
    return m.reshape(N_HEADS * KV_LORA, D_MODEL).astype(BF16)


def _pack_table(t):
    bits = lax.bitcast_convert_type(t.astype(BF16), jnp.uint16).astype(jnp.uint32)
    half = t.shape[1] // 2
    return jnp.bitwise_or(jnp.left_shift(bits[:, :half], 16), bits[:, half:]).reshape(t.shape[0] * TAB_ROWS, 128)


def _rope_tables(pos):
    half = ROPE_DIM // 2
    inv = ROPE_THETA ** (-jnp.arange(half, dtype=F32) / half)
    ang = pos.astype(F32)[:, None] * inv
    cos, sin = jnp.cos(ang), jnp.sin(ang)
    z16 = jnp.zeros_like(cos)
    z96 = jnp.zeros((pos.shape[0], 128 - ROPE_DIM), F32)
    return (jnp.concatenate([cos, cos, z96], axis=1), jnp.concatenate([z16, sin, z96], axis=1),
            jnp.concatenate([-sin, z16, z96], axis=1))


def _t5_bucket(dist):
    n = jnp.maximum(dist, 0)
    exact = N_BUCKETS // 2
    big = exact + (jnp.log(jnp.maximum(n, 1).astype(F32) / exact) / math.log(MAX_DISTANCE / exact)
                   * (N_BUCKETS - exact)).astype(I32)
    return jnp.where(n < exact, n, jnp.minimum(big, N_BUCKETS - 1))


def _prompt_bias_tiles(rel_bias, t):
    i = jnp.arange(t)
    tiles = []
    for delta in range(3):
        d = delta * t + i[:, None] - i[None, :]
        tiles.append(jnp.moveaxis(rel_bias[_t5_bucket(d)], -1, 0))
    return jnp.stack(tiles).astype(F32)


def _sample_bias(rel_bias, past_len, n_tok, n_pages):
    kpos = jnp.arange((n_pages + 1) * 128)
    d = past_len + jnp.arange(n_tok)[:, None] - kpos[None, :]
    b = jnp.moveaxis(rel_bias[_t5_bucket(d)], -1, 0)
    return b.reshape(N_HEADS * n_tok, (n_pages + 1) * 128).astype(F32)


def _pad_rows(x, rows):
    return jnp.pad(x, ((0, 0), (0, rows - x.shape[1]), (0, 0)))


def _keys_on_lanes(x, n_seq, n_tok):
    xt = x.reshape(n_seq, n_tok, -1).transpose(0, 2, 1)
    return jnp.pad(xt, ((0, 0), (0, 0), (0, 128 - n_tok)))


def kernel(x_prompt, x_sample, cache_a_k, cache_a_v, cache_a_idx, cache_b_ckv, cache_b_krope, cache_c_k,
           cache_c_v, page_table, p_prompt, p_sample, rel_bias, g_mix, w_in, g_q, w_uq, g_kv, w_uk, w_uv,
           w_br_a, w_br_b, w_br_c, w_o, g_ffn, w_pq, sub_k1, sub_k2, peer_u, peer_v, g_ple, w_pg, w_pp,
           g_final):
    batch, s_len, _ = x_prompt.shape
    n_seq, n_tok, _ = x_sample.shape
    depth = g_mix.shape[0]
    n_pages, page = page_table.shape[1], cache_a_k.shape[2]
    past_len = n_pages * page
    n_p, n_s = batch * s_len, n_seq * n_tok
    tm = 256
    t_att = 256
    t_sel = 128
    g_pages = math.gcd(32, n_pages)
    tb_peer = 128

    h = jnp.concatenate([x_prompt.reshape(n_p, D_MODEL), x_sample.reshape(n_s, D_MODEL)], axis=0)
    p_all = jnp.concatenate([p_prompt.reshape(depth, n_p, PLE_DIM), p_sample.reshape(depth, n_s, PLE_DIM)], axis=1)
    pos_rows = jnp.concatenate([jnp.arange(s_len), jnp.tile(past_len + jnp.arange(n_tok), tm // n_tok)])
    rope_tabs = _rope_tables(pos_rows)
    bias_tiles = _prompt_bias_tiles(rel_bias, t_att)
    bias_s = _sample_bias(rel_bias, past_len, n_tok, n_pages)
    t_a_k, t_a_v, t_a_idx, t_b_kr = (jnp.swapaxes(c, 2, 3) for c in (cache_a_k, cache_a_v, cache_a_idx, cache_b_krope))
    t_c_k, t_c_v = (jnp.transpose(c, (0, 1, 3, 4, 2)).reshape(c.shape[:2] + (C_KV_HEADS * HEAD_DIM, page))
                    for c in (cache_c_k, cache_c_v))
    row2 = lambda g: g.reshape(1, -1)

    rows_p = [[] for _ in range(7)]
    rows_s = [[] for _ in range(7)]
    y = None
    for i in range(depth):
        (aq, ak, av, iq, ik, iw, qb, ckv, kr, kb, cq, ck, cv, gates) = _project(
            h, row2(g_mix[i]), _pack_w_in(w_in[i]), row2(g_q[i]), _pack_wq2(w_uq[i], w_uk[i]), row2(g_kv[i]),
            rope_tabs, n_p // tm, s_len // tm, tm)
        for lst, r in zip(rows_p, (ak, av, ik, ckv, kr, ck, cv)):
            lst.append(r[:n_p])
        for lst, r in zip(rows_s, (ak, av, ik, ckv, kr, ck, cv)):
            lst.append(r[n_p:])

        mask = _dsa_select(iq, iw, ik, batch, s_len, t_sel)
        oa_p = _dsa_attn(aq, ak, av, mask, bias_tiles, batch, s_len, t_att)
        ob_p = _mla_attn(qb, kb, batch, s_len, t_att)
        oc_p = _sb_attn(cq, ck, cv, batch, s_len, t_att)

        def heads_first(x, width):
            return x[n_p:].reshape(n_seq, n_tok, N_HEADS, width).transpose(0, 2, 1, 3).reshape(
                n_seq, N_HEADS * n_tok, width)

        new_t = lambda x: _keys_on_lanes(x[n_p:], n_seq, n_tok)
        addmask = _samp_select(page_table, heads_first(iq, IDX_DIM), heads_first(iw, 1), new_t(ik),
                               t_a_idx, i, g_pages)
        oa_s = _samp_dsa(page_table, heads_first(aq, HEAD_DIM), addmask, bias_s, new_t(ak), new_t(av),
                         t_a_k, t_a_v, i, g_pages)
        ob_s = _samp_mla(page_table, heads_first(qb, 256), _pad_rows(kb[n_p:].reshape(n_seq, n_tok, -1), 128),
                         cache_b_ckv, t_b_kr, i, g_pages)
        cq_s = cq[n_p:].reshape(n_seq, n_tok, C_KV_HEADS, C_GROUP, HEAD_DIM).transpose(0, 2, 3, 1, 4).reshape(
            n_seq, C_KV_HEADS, C_GROUP * n_tok, HEAD_DIM)
        oc_s = _samp_sb(page_table, cq_s, new_t(ck), new_t(cv), t_c_k, t_c_v, i, g_pages, n_tok)

        def tokens_first(x, width):
            return x.reshape(n_seq, N_HEADS, n_tok, width).transpose(0, 2, 1, 3).reshape(
                n_s, N_HEADS * width).astype(BF16)

        oa = jnp.concatenate([oa_p, tokens_first(oa_s, HEAD_DIM)], axis=0)
        ob = jnp.concatenate([ob_p, tokens_first(ob_s, KV_LORA)], axis=0)
        oc = jnp.concatenate([oc_p, tokens_first(oc_s, HEAD_DIM)], axis=0)

        h1, n2, s_keys = _merge(h, oa, ob, oc, gates, w_br_a[i].astype(BF16), _pack_wb(w_uv[i], w_br_b[i]),
                                w_br_c[i].astype(BF16), w_o[i].astype(BF16), row2(g_ffn[i]),
                                w_pq[i].astype(BF16), sub_k1[i].astype(BF16), sub_k2[i].astype(BF16), tm)
        e_t, g_t = _peer_topk(s_keys, math.gcd(512, n_p + n_s))
        e_tok = e_t.T * TAB_ROWS
        w_t = _peer_u(e_tok, n2.reshape(-1, 8, 128), g_t, _pack_table(peer_u[i]), tb_peer)
        po = _peer_v(e_tok, w_t, _pack_table(peer_v[i]), tb_peer)
        h, y = _ple(h1, po.reshape(-1, D_MODEL), p_all[i], row2(g_ple[i]), w_pg[i].astype(BF16),
                    w_pp[i].astype(BF16), row2(g_final), tm)

    def stack_rows(lst, lead, tail):
        return jnp.stack(lst, axis=0).reshape((depth,) + lead + tail)

    tails = [(HEAD_DIM,), (HEAD_DIM,), (IDX_DIM,), (KV_LORA,), (ROPE_DIM,), (C_KV_HEADS, HEAD_DIM),
             (C_KV_HEADS, HEAD_DIM)]
    out_p = [stack_rows(l, (batch, s_len), t) for l, t in zip(rows_p, tails)]
    out_s = [stack_rows(l, (n_seq, n_tok), t) for l, t in zip(rows_s, tails)]
    return (y[:n_p].reshape(batch, s_len, D_MODEL), y[n_p:].reshape(n_seq, n_tok, D_MODEL), *out_p, *out_s)
```
